```python
import math
import jax, jax.numpy as jnp
from jax import lax
import numpy as np

D_MODEL = 1024
BATCH = 4
SEQ = 4096
DEPTH = 4

CHUNK = 64
EPS = 1e-6
CONV_WIDTH = 3
CONV_DIM = D_MODEL // 2
RET_HEADS = 4
RET_HEAD_DIM = (D_MODEL // 2) // RET_HEADS
RET_DIM = RET_HEADS * RET_HEAD_DIM
ROPE_THETA = 10000.0
SB_HEADS = 8
SB_HEAD_DIM = D_MODEL // SB_HEADS
SB_DIM = SB_HEADS * SB_HEAD_DIM
SB_BLOCK = 128
FFN_MULT_OF = 256
D_FF = -(-8 * D_MODEL // (3 * FFN_MULT_OF)) * FFN_MULT_OF
EVEN_IN = 3 * CONV_DIM + 4 * RET_DIM
N_EVEN = (DEPTH + 1) // 2
N_ODD = DEPTH // 2

kernel_name = 'hybrid_conv_retention_stickbreaking_adaln_trunk'


def rms_norm(x, g):
    xf = x.astype(jnp.float32)
    y = xf * lax.rsqrt(jnp.mean(xf * xf, axis=-1, keepdims=True) + EPS)
    return (y * g.astype(jnp.float32)).astype(x.dtype)


def rotary(x, pos):
    dh = x.shape[-1]
    inv_freq = 1.0 / (ROPE_THETA ** (jnp.arange(0, dh, 2, dtype=jnp.float32) / dh))
    ang = pos.astype(jnp.float32)[:, None] * inv_freq[None, :]
    cos = jnp.cos(ang)[None, :, None, :].astype(x.dtype)
    sin = jnp.sin(ang)[None, :, None, :].astype(x.dtype)
    x1, x2 = jnp.split(x, 2, axis=-1)
    return jnp.concatenate([x1 * cos - x2 * sin, x1 * sin + x2 * cos], axis=-1)


def short_gated_conv(b_gate, c_gate, u, conv_w):
    z = c_gate * u
    w = conv_w[:, None, :].astype(z.dtype)
    y = lax.conv_general_dilated(z, w, window_strides=(1,), padding=[(CONV_WIDTH - 1, 0)],
                                 dimension_numbers=('NWC', 'WIO', 'NWC'),
                                 feature_group_count=CONV_DIM)
    return b_gate * y


def retention(q, k, v):
    bsz, s_len, h, dh = q.shape
    n = s_len // CHUNK
    dt = q.dtype
    log_g = jnp.log1p(-jnp.exp2(-5.0 - jnp.arange(h, dtype=jnp.float32)))
    idx = jnp.arange(CHUNK, dtype=jnp.float32)
    intra_dec = jnp.exp(jnp.abs(idx[:, None] - idx[None, :])[None] * log_g[:, None, None]).astype(dt)
    q_dec = jnp.exp((idx + 1.0)[None, :] * log_g[:, None]).astype(dt)
    k_dec = jnp.exp((CHUNK - 1.0 - idx)[None, :] * log_g[:, None]).astype(dt)
    chunk_dec = jnp.exp(CHUNK * log_g).astype(dt)
    qc = q.reshape(bsz, n, CHUNK, h, dh)
    kc = k.reshape(bsz, n, CHUNK, h, dh)
    vc = v.reshape(bsz, n, CHUNK, h, dh)
    scores = jnp.einsum('bnihd,bnjhd->bnhij', qc, kc) * intra_dec
    o_intra = jnp.einsum('bnhij,bnjhd->bnihd', scores, vc)
    kv = jnp.einsum('bnjhd,hj,bnjhe->bnhde', kc, k_dec, vc)

    def step(state, kv_i):
        return state * chunk_dec[None, :, None, None] + kv_i, state

    _, prev = lax.scan(step, jnp.zeros_like(kv[:, 0]), jnp.moveaxis(kv, 1, 0))
    prev = jnp.moveaxis(prev, 0, 1)
    o_inter = jnp.einsum('bnihd,hi,bnhde->bnihe', qc, q_dec, prev)
    return (o_intra + o_inter).reshape(bsz, s_len, h, dh)


def stick_breaking(q, k, v):
    s_len, dh = q.shape[2], q.shape[3]
    scale = dh ** -0.5
    outs = []
    for i in range(s_len // SB_BLOCK):
        q0 = i * SB_BLOCK
        kl = q0 + SB_BLOCK
        qb, kb, vb = q[:, :, q0:kl], k[:, :, :kl], v[:, :, :kl]
        z = jnp.einsum('bhqd,bhkd->bhqk', qb, kb).astype(jnp.float32) * scale
        qpos = q0 + jnp.arange(SB_BLOCK)
        kpos = jnp.arange(kl)
        mask = kpos[None, :] < qpos[:, None]
        log_beta = jax.nn.log_sigmoid(z)
        log_keep = jnp.where(mask, jax.nn.log_sigmoid(-z), 0.0)
        acc = lax.cumsum(log_keep, axis=3, reverse=True) - log_keep
        w = jnp.where(mask, jnp.exp(log_beta + acc), 0.0)
        outs.append(jnp.einsum('bhqk,bhkd->bhqd', w.astype(vb.dtype), vb))
    return jnp.concatenate(outs, axis=2)


def even_mixer(h, w_in, conv_w, ret_norm_g, w_out):
    bsz, s_len, _ = h.shape
    proj = h @ w_in
    cuts = [CONV_DIM, 2 * CONV_DIM, 3 * CONV_DIM, 3 * CONV_DIM + RET_DIM,
            3 * CONV_DIM + 2 * RET_DIM, 3 * CONV_DIM + 3 * RET_DIM]
    b_gate, c_gate, u, q, k, v, g = jnp.split(proj, cuts, axis=-1)
    a_out = short_gated_conv(b_gate, c_gate, u, conv_w)
    pos = jnp.arange(s_len)
    q = rotary(q.reshape(bsz, s_len, RET_HEADS, RET_HEAD_DIM), pos)
    k = rotary(k.reshape(bsz, s_len, RET_HEADS, RET_HEAD_DIM), pos) * (RET_HEAD_DIM ** -0.5)
    v = v.reshape(bsz, s_len, RET_HEADS, RET_HEAD_DIM)
    r = retention(q, k, v)
    r = rms_norm(r, ret_norm_g.reshape(RET_HEADS, RET_HEAD_DIM))
    r = jax.nn.silu(g) * r.reshape(bsz, s_len, RET_DIM)
    return jnp.concatenate([a_out, r], axis=-1) @ w_out


def odd_mixer(h, w_qkv, q_norm_g, k_norm_g, w_out):
    bsz, s_len, _ = h.shape
    q, k, v = jnp.split(h @ w_qkv, 3, axis=-1)
    q = rms_norm(q.reshape(bsz, s_len, SB_HEADS, SB_HEAD_DIM), q_norm_g)
    k = rms_norm(k.reshape(bsz, s_len, SB_HEADS, SB_HEAD_DIM), k_norm_g)
    v = v.reshape(bsz, s_len, SB_HEADS, SB_HEAD_DIM)
    o = stick_breaking(q.transpose(0, 2, 1, 3), k.transpose(0, 2, 1, 3), v.transpose(0, 2, 1, 3))
    return o.transpose(0, 2, 1, 3).reshape(bsz, s_len, SB_DIM) @ w_out


def swiglu(h, w_gate, w_up, w_down):
    return (jax.nn.silu(h @ w_gate) * (h @ w_up)) @ w_down


def setup_inputs(seed: int = 0) -> dict:
    key = jax.random.key(seed)
    ks = jax.random.split(key, 17)

    def nrm(k, shape, fan_in, mult=1.0):
        return jax.random.normal(k, shape, jnp.float32) * (mult * fan_in ** -0.5)

    def gain(k, shape):
        return 1.0 + 0.02 * jax.random.normal(k, shape, jnp.float32)

    return {
        'x': jax.random.normal(ks[0], (BATCH, SEQ, D_MODEL), jnp.float32),
        'c': jax.random.normal(ks[1], (BATCH, D_MODEL), jnp.float32),
        'ada_w': nrm(ks[2], (DEPTH, D_MODEL, 6 * D_MODEL), D_MODEL, 0.5),
        'ada_b': 0.02 * jax.random.normal(ks[3], (DEPTH, 6 * D_MODEL), jnp.float32),
        'norm_mix_g': gain(ks[4], (DEPTH, D_MODEL)),
        'norm_ffn_g': gain(ks[5], (DEPTH, D_MODEL)),
        'ev_w_in': nrm(ks[6], (N_EVEN, D_MODEL, EVEN_IN), D_MODEL),
        'ev_conv_w': nrm(ks[7], (N_EVEN, CONV_WIDTH, CONV_DIM), CONV_WIDTH),
        'ev_ret_norm_g': gain(ks[8], (N_EVEN, RET_DIM)),
        'ev_w_out': nrm(ks[9], (N_EVEN, CONV_DIM + RET_DIM, D_MODEL), CONV_DIM + RET_DIM),
        'od_w_qkv': nrm(ks[10], (N_ODD, D_MODEL, 3 * SB_DIM), D_MODEL),
        'od_q_norm_g': gain(ks[11], (N_ODD, SB_HEAD_DIM)),
        'od_k_norm_g': gain(ks[12], (N_ODD, SB_HEAD_DIM)),
        'od_w_out': nrm(ks[13], (N_ODD, SB_DIM, D_MODEL), SB_DIM),
        'ffn_w_gate': nrm(ks[14], (DEPTH, D_MODEL, D_FF), D_MODEL),
        'ffn_w_up': nrm(ks[15], (DEPTH, D_MODEL, D_FF), D_MODEL),
        'ffn_w_down': nrm(ks[16], (DEPTH, D_FF, D_MODEL), D_FF),
    }


def reference(x, c, ada_w, ada_b, norm_mix_g, norm_ffn_g, ev_w_in, ev_conv_w, ev_ret_norm_g,
              ev_w_out, od_w_qkv, od_q_norm_g, od_k_norm_g, od_w_out, ffn_w_gate, ffn_w_up,
              ffn_w_down):
    c_act = jax.nn.silu(c)
    for l in range(DEPTH):
        mod = c_act @ ada_w[l] + ada_b[l]
        sh1, sc1, g1, sh2, sc2, g2 = [m[:, None, :] for m in jnp.split(mod, 6, axis=-1)]
        h = rms_norm(x, norm_mix_g[l]) * (1 + sc1) + sh1
        j = l // 2
        if l % 2 == 0:
            y = even_mixer(h, ev_w_in[j], ev_conv_w[j], ev_ret_norm_g[j], ev_w_out[j])
        else:
            y = odd_mixer(h, od_w_qkv[j], od_q_norm_g[j], od_k_norm_g[j], od_w_out[j])
        x = x + g1 * y
        h = rms_norm(x, norm_ffn_g[l]) * (1 + sc2) + sh2
        x = x + g2 * swiglu(h, ffn_w_gate[l], ffn_w_up[l], ffn_w_down[l])
    return x
```

```python
import functools
import math

import jax
import jax.numpy as jnp
import numpy as np
from jax import lax
from jax.experimental import pallas as pl
from jax.experimental.pallas import tpu as pltpu

D_MODEL = 1024
BATCH = 4
SEQ = 4096
DEPTH = 4
CHUNK = 64
EPS = 1e-6
CONV_WIDTH = 3
CONV_DIM = 512
RET_HEADS = 4
RET_HEAD_DIM = 128
RET_DIM = 512
ROPE_THETA = 10000.0
SB_HEADS = 8
SB_HEAD_DIM = 128
SB_DIM = 1024
D_FF = 2816
EVEN_IN = 3 * CONV_DIM + 4 * RET_DIM

V7X_SUBLANES = 8
V7X_LANES = 128
V7X_MXU_DIM = 256
V7X_VMEM_BYTES = 64 * 1024 * 1024

LOG2E = 1.0 / math.log(2.0)

EVEN_TM = 512
RET_TILE = V7X_MXU_DIM
QKV_TM = 512
POST_TM = 256
SB_TQ = V7X_MXU_DIM
SB_TK = V7X_MXU_DIM

F32 = jnp.float32
BF16 = jnp.bfloat16


def _sigmoid(x):
    return 1.0 / (1.0 + jnp.exp(-x))


def _norm_mod(x, gain, scale, shift):
    ms = jnp.mean(x * x, axis=-1, keepdims=True)
    return (x * lax.rsqrt(ms + EPS)) * (gain * (1.0 + scale)) + shift


def _mod_row(mod_ref, k, b):
    return mod_ref[0, k, pl.ds(b, 1), :]


def _adaln_kernel(c_ref, w_ref, b_ref, o_ref):
    c = c_ref[...]
    ca = c * _sigmoid(c)
    o_ref[0, 0] = jnp.dot(ca, w_ref[0], precision=lax.Precision.HIGHEST,
                          preferred_element_type=F32) + b_ref[0, 0]


def _adaln(c, ada_w, ada_b):
    c_pad = jnp.pad(c, ((0, V7X_SUBLANES - BATCH), (0, 0)))
    b4 = ada_b.reshape(DEPTH, 6, 1, D_MODEL)
    return pl.pallas_call(
        _adaln_kernel,
        grid=(DEPTH, 6),
        in_specs=[
            pl.BlockSpec((V7X_SUBLANES, D_MODEL), lambda l, k: (0, 0)),
            pl.BlockSpec((1, D_MODEL, D_MODEL), lambda l, k: (l, 0, k)),
            pl.BlockSpec((1, 1, 1, D_MODEL), lambda l, k: (l, k, 0, 0)),
        ],
        out_specs=pl.BlockSpec((1, 1, V7X_SUBLANES, D_MODEL), lambda l, k: (l, k, 0, 0)),
        out_shape=jax.ShapeDtypeStruct((DEPTH, 6, V7X_SUBLANES, D_MODEL), F32),
        compiler_params=pltpu.CompilerParams(dimension_semantics=("parallel", "parallel")),
        name="adaln",
    )(c_pad, ada_w, b4)


def _retention_tables():
    h = np.arange(RET_HEADS, dtype=np.float64)
    log_g = np.log1p(-np.exp2(-5.0 - h))
    idx = np.arange(RET_TILE, dtype=np.float64)
    diff = idx[:, None] - idx[None, :]
    same_or_earlier = (idx[None, :] // CHUNK) <= (idx[:, None] // CHUNK)
    k_scale = RET_HEAD_DIM ** -0.5
    dec = np.exp(np.abs(diff)[None] * log_g[:, None, None]) * same_or_earlier[None] * k_scale
    qdec = np.exp((idx + 1.0)[None, :] * log_g[:, None])
    kdec = np.exp((RET_TILE - 1.0 - idx)[None, :] * log_g[:, None]) * k_scale
    sdec = np.exp(RET_TILE * log_g)
    qdec = np.broadcast_to(qdec[:, :, None], (RET_HEADS, RET_TILE, V7X_LANES))
    kdec = np.broadcast_to(kdec[:, :, None], (RET_HEADS, RET_TILE, V7X_LANES))
    sdec = np.broadcast_to(sdec[:, None, None], (RET_HEADS, V7X_SUBLANES, V7X_LANES))
    inv_freq = 1.0 / (ROPE_THETA ** (np.arange(0, RET_HEAD_DIM, 2, dtype=np.float64) / RET_HEAD_DIM))
    ang = np.arange(SEQ, dtype=np.float64)[:, None] * inv_freq[None, :]
    cos2 = np.concatenate([np.cos(ang), np.cos(ang)], axis=-1)
    sin2 = np.concatenate([-np.sin(ang), np.sin(ang)], axis=-1)
    rot = np.stack([cos2, sin2])
    f = lambda a: jnp.asarray(np.ascontiguousarray(a), dtype=F32)
    return f(dec), f(qdec), f(kdec), f(sdec), f(rot)


def _even_kernel(x_ref, mod_ref, ng_ref, win_ref, cw_ref, rg_ref, rot_ref, dec_ref, qdec_ref,
                 kdec_ref, sdec_ref, mix_ref, state_ref, zbuf_ref):
    b = pl.program_id(0)
    si = pl.program_id(1)
    tm = EVEN_TM

    @pl.when(si == 0)
    def _():
        state_ref[...] = jnp.zeros_like(state_ref)
        zbuf_ref[0:V7X_SUBLANES, :] = jnp.zeros((V7X_SUBLANES, CONV_DIM), F32)

    x = x_ref[0]
    h = _norm_mod(x, ng_ref[...], _mod_row(mod_ref, 1, b), _mod_row(mod_ref, 0, b)).astype(BF16)
    proj = jnp.dot(h, win_ref[...], preferred_element_type=F32)

    b_gate = proj[:, 0:CONV_DIM]
    z = proj[:, CONV_DIM:2 * CONV_DIM] * proj[:, 2 * CONV_DIM:3 * CONV_DIM]
    zbuf_ref[V7X_SUBLANES:V7X_SUBLANES + tm, :] = z
    z1 = zbuf_ref[V7X_SUBLANES - 1:V7X_SUBLANES - 1 + tm, :]
    z2 = zbuf_ref[V7X_SUBLANES - 2:V7X_SUBLANES - 2 + tm, :]
    cw = cw_ref[...]
    y = cw[0:1, :] * z2 + cw[1:2, :] * z1 + cw[2:3, :] * z
    mix_ref[0, :, 0:CONV_DIM] = (b_gate * y).astype(BF16)
    zbuf_ref[0:V7X_SUBLANES, :] = zbuf_ref[tm:tm + V7X_SUBLANES, :]

    q_off = 3 * CONV_DIM
    k_off = q_off + RET_DIM
    v_off = k_off + RET_DIM
    g_off = v_off + RET_DIM
    for r in range(tm // RET_TILE):
        r0 = r * RET_TILE
        cos = rot_ref[0, r0:r0 + RET_TILE, :]
        sin = rot_ref[1, r0:r0 + RET_TILE, :]
        for hh in range(RET_HEADS):
            c0 = hh * RET_HEAD_DIM
            q = proj[r0:r0 + RET_TILE, q_off + c0:q_off + c0 + RET_HEAD_DIM]
            k = proj[r0:r0 + RET_TILE, k_off + c0:k_off + c0 + RET_HEAD_DIM]
            v = proj[r0:r0 + RET_TILE, v_off + c0:v_off + c0 + RET_HEAD_DIM].astype(BF16)
            g = proj[r0:r0 + RET_TILE, g_off + c0:g_off + c0 + RET_HEAD_DIM]
            qr = q * cos + pltpu.roll(q, RET_HEAD_DIM // 2, axis=1) * sin
            kr = k * cos + pltpu.roll(k, RET_HEAD_DIM // 2, axis=1) * sin
            s = lax.dot_general(qr.astype(BF16), kr.astype(BF16), (((1,), (1,)), ((), ())),
                                preferred_element_type=F32)
            s = s * dec_ref[hh]
            o = jnp.dot(s.astype(BF16), v, preferred_element_type=F32)
            st = state_ref[hh]
            o = o + jnp.dot((qr * qdec_ref[hh]).astype(BF16), st.astype(BF16),
                            preferred_element_type=F32)
            kd = (kr * kdec_ref[hh]).astype(BF16)
            kv = lax.dot_general(kd, v, (((0,), (0,)), ((), ())), preferred_element_type=F32)
            state_ref[hh] = st * sdec_ref[hh, 0:1, :] + kv
            ms = jnp.mean(o * o, axis=-1, keepdims=True)
            rn = (o * lax.rsqrt(ms + EPS)) * rg_ref[:, c0:c0 + RET_HEAD_DIM]
            out = (g * _sigmoid(g)) * rn
            mix_ref[0, r0:r0 + RET_TILE, CONV_DIM + c0:CONV_DIM + c0 + RET_HEAD_DIM] = out.astype(BF16)


def _even_mix(x, mod, l, norm_g, w_in, conv_w, ret_g, tables):
    dec, qdec, kdec, sdec, rot = tables
    tm = EVEN_TM
    cw = jnp.pad(conv_w, ((0, V7X_SUBLANES - CONV_WIDTH), (0, 0)))
    const2 = lambda b, s: (0, 0)
    const3 = lambda b, s: (0, 0, 0)
    return pl.pallas_call(
        _even_kernel,
        grid=(BATCH, SEQ // tm),
        in_specs=[
            pl.BlockSpec((1, tm, D_MODEL), lambda b, s: (b, s, 0)),
            pl.BlockSpec((1, 6, V7X_SUBLANES, D_MODEL), lambda b, s: (l, 0, 0, 0)),
            pl.BlockSpec((1, D_MODEL), const2),
            pl.BlockSpec((D_MODEL, EVEN_IN), const2),
            pl.BlockSpec((V7X_SUBLANES, CONV_DIM), const2),
            pl.BlockSpec((1, RET_DIM), const2),
            pl.BlockSpec((2, tm, RET_HEAD_DIM), lambda b, s: (0, s, 0)),
            pl.BlockSpec((RET_HEADS, RET_TILE, RET_TILE), const3),
            pl.BlockSpec((RET_HEADS, RET_TILE, V7X_LANES), const3),
            pl.BlockSpec((RET_HEADS, RET_TILE, V7X_LANES), const3),
            pl.BlockSpec((RET_HEADS, V7X_SUBLANES, V7X_LANES), const3),
        ],
        out_specs=pl.BlockSpec((1, tm, D_MODEL), lambda b, s: (b, s, 0)),
        out_shape=jax.ShapeDtypeStruct((BATCH, SEQ, D_MODEL), BF16),
        scratch_shapes=[
            pltpu.VMEM((RET_HEADS, RET_HEAD_DIM, RET_HEAD_DIM), F32),
            pltpu.VMEM((tm + V7X_SUBLANES, CONV_DIM), F32),
        ],
        compiler_params=pltpu.CompilerParams(
            dimension_semantics=("arbitrary", "arbitrary"),
            vmem_limit_bytes=56 * 1024 * 1024),
        name="even_mix",
    )(x, mod, norm_g.reshape(1, D_MODEL), w_in, cw, ret_g.reshape(1, RET_DIM), rot, dec, qdec,
      kdec, sdec)


def _qkv_kernel(x_ref, mod_ref, ng_ref, w_ref, qg_ref, kg_ref, q_ref, k_ref, v_ref):
    b = pl.program_id(0)
    x = x_ref[0]
    h = _norm_mod(x, ng_ref[...], _mod_row(mod_ref, 1, b), _mod_row(mod_ref, 0, b)).astype(BF16)
    qkv = jnp.dot(h, w_ref[...], preferred_element_type=F32)
    qg = qg_ref[...] * (SB_HEAD_DIM ** -0.5 * LOG2E)
    kg = kg_ref[...]
    for hh in range(SB_HEADS):
        c0 = hh * SB_HEAD_DIM
        q = qkv[:, c0:c0 + SB_HEAD_DIM]
        k = qkv[:, SB_DIM + c0:SB_DIM + c0 + SB_HEAD_DIM]
        v = qkv[:, 2 * SB_DIM + c0:2 * SB_DIM + c0 + SB_HEAD_DIM]
        qn = (q * lax.rsqrt(jnp.mean(q * q, axis=-1, keepdims=True) + EPS)) * qg
        kn = (k * lax.rsqrt(jnp.mean(k * k, axis=-1, keepdims=True) + EPS)) * kg
        q_ref[0, hh] = qn.astype(BF16)
        k_ref[0, hh] = kn.astype(BF16)
        v_ref[0, hh] = v.astype(BF16)


def _qkv(x, mod, l, norm_g, w_qkv, q_g, k_g):
    tm = QKV_TM
    const2 = lambda b, s: (0, 0)
    head_spec = pl.BlockSpec((1, SB_HEADS, tm, SB_HEAD_DIM), lambda b, s: (b, 0, s, 0))
    head_shape = jax.ShapeDtypeStruct((BATCH, SB_HEADS, SEQ, SB_HEAD_DIM), BF16)
    return pl.pallas_call(
        _qkv_kernel,
        grid=(BATCH, SEQ // tm),
        in_specs=[
            pl.BlockSpec((1, tm, D_MODEL), lambda b, s: (b, s, 0)),
            pl.BlockSpec((1, 6, V7X_SUBLANES, D_MODEL), lambda b, s: (l, 0, 0, 0)),
            pl.BlockSpec((1, D_MODEL), const2),
            pl.BlockSpec((D_MODEL, 3 * SB_DIM), const2),
            pl.BlockSpec((1, SB_HEAD_DIM), const2),
            pl.BlockSpec((1, SB_HEAD_DIM), const2),
        ],
        out_specs=[head_spec, head_spec, head_spec],
        out_shape=[head_shape, head_shape, head_shape],
        compiler_params=pltpu.CompilerParams(
            dimension_semantics=("parallel", "parallel"),
            vmem_limit_bytes=56 * 1024 * 1024),
        name="qkv",
    )(x, mod, norm_g.reshape(1, D_MODEL), w_qkv, q_g.reshape(1, SB_HEAD_DIM),
      k_g.reshape(1, SB_HEAD_DIM))


def _softplus2(z):
    return jnp.maximum(z, 0.0) + jnp.log(1.0 + jnp.exp2(-jnp.abs(z))) * LOG2E


def _sb_kernel(q_ref, k_ref, v_ref, o_ref):
    qi = pl.program_id(2)
    q = q_ref[0, 0]
    row = lax.broadcasted_iota(jnp.int32, (SB_TK, SB_TK), 0)
    col = lax.broadcasted_iota(jnp.int32, (SB_TK, SB_TK), 1)
    later = (row > col).astype(BF16)
    causal = col < row

    def block(j):
        start = pl.multiple_of(j * SB_TK, SB_TK)
        kb = k_ref[0, 0, pl.ds(start, SB_TK), :]
        vb = v_ref[0, 0, pl.ds(start, SB_TK), :]
        z = lax.dot_general(q, kb, (((1,), (1,)), ((), ())), preferred_element_type=F32)
        return z, vb

    z, vb = block(qi)
    sp = jnp.where(causal, _softplus2(z), 0.0)
    cs = jnp.dot(sp.astype(BF16), later, preferred_element_type=F32)
    w = jnp.where(causal, jnp.exp2(z - sp - cs), 0.0)
    acc = jnp.dot(w.astype(BF16), vb, preferred_element_type=F32)
    r = jnp.sum(sp, axis=1, keepdims=True)

    def body(i, carry):
        acc, r = carry
        z, vb = block(qi - 1 - i)
        sp = _softplus2(z)
        cs = jnp.dot(sp.astype(BF16), later, preferred_element_type=F32)
        w = jnp.exp2(z - sp - cs - r)
        acc = acc + jnp.dot(w.astype(BF16), vb, preferred_element_type=F32)
        return acc, r + jnp.sum(sp, axis=1, keepdims=True)

    acc, r = lax.fori_loop(0, qi, body, (acc, r))
    o_ref[0] = acc.astype(BF16)


def _sb_attention(q, k, v):
    assert SB_TQ == SB_TK
    kv_spec = pl.BlockSpec((1, 1, SEQ, SB_HEAD_DIM), lambda b, h, i: (b, h, 0, 0))
    return pl.pallas_call(
        _sb_kernel,
        grid=(BATCH, SB_HEADS, SEQ // SB_TQ),
        in_specs=[
            pl.BlockSpec((1, 1, SB_TQ, SB_HEAD_DIM), lambda b, h, i: (b, h, i, 0)),
            kv_spec, kv_spec,
        ],
        out_specs=pl.BlockSpec((1, SB_TQ, SB_HEAD_DIM), lambda b, h, i: (b, i, h)),
        out_shape=jax.ShapeDtypeStruct((BATCH, SEQ, SB_DIM), BF16),
        compiler_params=pltpu.CompilerParams(
            dimension_semantics=("parallel", "parallel", "parallel")),
        name="sb_attn",
    )(q, k, v)


def _post_kernel(x_ref, mix_ref, mod_ref, wo_ref, ng_ref, wg_ref, wu_ref, wd_ref, o_ref):
    b = pl.program_id(0)
    x = x_ref[0]
    y = jnp.dot(mix_ref[0], wo_ref[...], preferred_element_type=F32)
    x1 = x + _mod_row(mod_ref, 2, b) * y
    h = _norm_mod(x1, ng_ref[...], _mod_row(mod_ref, 4, b), _mod_row(mod_ref, 3, b)).astype(BF16)
    gate = jnp.dot(h, wg_ref[...], preferred_element_type=F32)
    up = jnp.dot(h, wu_ref[...], preferred_element_type=F32)
    act = ((gate * _sigmoid(gate)) * up).astype(BF16)
    ffn = jnp.dot(act, wd_ref[...], preferred_element_type=F32)
    o_ref[0] = x1 + _mod_row(mod_ref, 5, b) * ffn


def _post(x, mix, mod, l, w_out, norm_g, w_gate, w_up, w_down):
    tm = POST_TM
    const2 = lambda b, s: (0, 0)
    return pl.pallas_call(
        _post_kernel,
        grid=(BATCH, SEQ // tm),
        in_specs=[
            pl.BlockSpec((1, tm, D_MODEL), lambda b, s: (b, s, 0)),
            pl.BlockSpec((1, tm, D_MODEL), lambda b, s: (b, s, 0)),
            pl.BlockSpec((1, 6, V7X_SUBLANES, D_MODEL), lambda b, s: (l, 0, 0, 0)),
            pl.BlockSpec((D_MODEL, D_MODEL), const2),
            pl.BlockSpec((1, D_MODEL), const2),
            pl.BlockSpec((D_MODEL, D_FF), const2),
            pl.BlockSpec((D_MODEL, D_FF), const2),
            pl.BlockSpec((D_FF, D_MODEL), const2),
        ],
        out_specs=pl.BlockSpec((1, tm, D_MODEL), lambda b, s: (b, s, 0)),
        out_shape=jax.ShapeDtypeStruct((BATCH, SEQ, D_MODEL), F32),
        compiler_params=pltpu.CompilerParams(
            dimension_semantics=("parallel", "parallel"),
            vmem_limit_bytes=58 * 1024 * 1024),
        name="post",
    )(x, mix, mod, w_out, norm_g.reshape(1, D_MODEL), w_gate, w_up, w_down)


def kernel(x, c, ada_w, ada_b, norm_mix_g, norm_ffn_g, ev_w_in, ev_conv_w, ev_ret_norm_g, ev_w_out,
           od_w_qkv, od_q_norm_g, od_k_norm_g, od_w_out, ffn_w_gate, ffn_w_up, ffn_w_down):
    bf = lambda w: w.astype(BF16)
    ev_w_in, ev_w_out, od_w_qkv, od_w_out = bf(ev_w_in), bf(ev_w_out), bf(od_w_qkv), bf(od_w_out)
    ffn_w_gate, ffn_w_up, ffn_w_down = bf(ffn_w_gate), bf(ffn_w_up), bf(ffn_w_down)
    mod = _adaln(c, ada_w, ada_b)
    tables = _retention_tables()
    for l in range(DEPTH):
        j = l // 2
        if l % 2 == 0:
            mix = _even_mix(x, mod, l, norm_mix_g[l], ev_w_in[j], ev_conv_w[j], ev_ret_norm_g[j],
                            tables)
            w_out = ev_w_out[j]
        else:
            q, k, v = _qkv(x, mod, l, norm_mix_g[l], od_w_qkv[j], od_q_norm_g[j], od_k_norm_g[j])
            mix = _sb_attention(q, k, v)
            w_out = od_w_out[j]
        x = _post(x, mix, mod, l, w_out, norm_ffn_g[l], ffn_w_gate[l], ffn_w_up[l], ffn_w_down[l])
    return x
```

```python
import functools
import math

import jax
import jax.numpy as jnp
import numpy as np
from jax import lax
from jax.experimental import pallas as pl
from jax.experimental.pallas import tpu as pltpu

D_MODEL = 1024
BATCH = 4
SEQ = 4096
DEPTH = 4
CHUNK = 64
EPS = 1e-6
CONV_WIDTH = 3
CONV_DIM = 512
RET_HEADS = 4
RET_HEAD_DIM = 128
RET_DIM = 512
ROPE_THETA = 10000.0
SB_HEADS = 8
SB_HEAD_DIM = 128
SB_DIM = 1024
D_FF = 2816
EVEN_IN = 3 * CONV_DIM + 4 * RET_DIM

V7X_SUBLANES = 8
V7X_LANES = 128
V7X_MXU_DIM = 256
V7X_VMEM_BYTES = 64 * 1024 * 1024

LOG2E = 1.0 / math.log(2.0)

EVEN_TM = 512
RET_TILE = V7X_MXU_DIM
QKV_TM = 512
POST_TM = 256
SB_TK = V7X_MXU_DIM
SB_NSUB = 2
SB_TQ = SB_NSUB * SB_TK
MASKED_LOGIT = -1e30

F32 = jnp.float32
BF16 = jnp.bfloat16


def _sigmoid(x):
    return 1.0 / (1.0 + jnp.exp(-x))


def _norm_mod(x, gain, scale, shift):
    ms = jnp.mean(x * x, axis=-1, keepdims=True)
    return (x * lax.rsqrt(ms + EPS)) * (gain * (1.0 + scale)) + shift


def _mod_row(mod_ref, k, b):
    return mod_ref[0, k, pl.ds(b, 1), :]


def _adaln_kernel(c_ref, w_ref, b_ref, o_ref):
    c = c_ref[...]
    ca = c * _sigmoid(c)
    o_ref[0, 0] = jnp.dot(ca, w_ref[0], precision=lax.Precision.HIGHEST,
                          preferred_element_type=F32) + b_ref[0, 0]


def _adaln(c, ada_w, ada_b):
    c_pad = jnp.pad(c, ((0, V7X_SUBLANES - BATCH), (0, 0)))
    b4 = ada_b.reshape(DEPTH, 6, 1, D_MODEL)
    return pl.pallas_call(
        _adaln_kernel,
        grid=(DEPTH, 6),
        in_specs=[
            pl.BlockSpec((V7X_SUBLANES, D_MODEL), lambda l, k: (0, 0)),
            pl.BlockSpec((1, D_MODEL, D_MODEL), lambda l, k: (l, 0, k)),
            pl.BlockSpec((1, 1, 1, D_MODEL), lambda l, k: (l, k, 0, 0)),
        ],
        out_specs=pl.BlockSpec((1, 1, V7X_SUBLANES, D_MODEL), lambda l, k: (l, k, 0, 0)),
        out_shape=jax.ShapeDtypeStruct((DEPTH, 6, V7X_SUBLANES, D_MODEL), F32),
        compiler_params=pltpu.CompilerParams(dimension_semantics=("parallel", "parallel")),
        name="adaln",
    )(c_pad, ada_w, b4)


def _retention_tables():
    h = np.arange(RET_HEADS, dtype=np.float64)
    log_g = np.log1p(-np.exp2(-5.0 - h))
    idx = np.arange(RET_TILE, dtype=np.float64)
    diff = idx[:, None] - idx[None, :]
    same_or_earlier = (idx[None, :] // CHUNK) <= (idx[:, None] // CHUNK)
    k_scale = RET_HEAD_DIM ** -0.5
    dec = np.exp(np.abs(diff)[None] * log_g[:, None, None]) * same_or_earlier[None] * k_scale
    qdec = np.exp((idx + 1.0)[None, :] * log_g[:, None])
    kdec = np.exp((RET_TILE - 1.0 - idx)[None, :] * log_g[:, None]) * k_scale
    sdec = np.exp(RET_TILE * log_g)
    qdec = np.broadcast_to(qdec[:, :, None], (RET_HEADS, RET_TILE, V7X_LANES))
    kdec = np.broadcast_to(kdec[:, :, None], (RET_HEADS, RET_TILE, V7X_LANES))
    sdec = np.broadcast_to(sdec[:, None, None], (RET_HEADS, V7X_SUBLANES, V7X_LANES))
    inv_freq = 1.0 / (ROPE_THETA ** (np.arange(0, RET_HEAD_DIM, 2, dtype=np.float64) / RET_HEAD_DIM))
    ang = np.arange(SEQ, dtype=np.float64)[:, None] * inv_freq[None, :]
    cos2 = np.concatenate([np.cos(ang), np.cos(ang)], axis=-1)
    sin2 = np.concatenate([-np.sin(ang), np.sin(ang)], axis=-1)
    rot = np.stack([cos2, sin2])
    f = lambda a: jnp.asarray(np.ascontiguousarray(a), dtype=F32)
    return f(dec), f(qdec), f(kdec), f(sdec), f(rot)


def _even_kernel(x_ref, mod_ref, ng_ref, win_ref, cw_ref, rg_ref, rot_ref, dec_ref, qdec_ref,
                 kdec_ref, sdec_ref, mix_ref, state_ref, zbuf_ref):
    b = pl.program_id(0)
    si = pl.program_id(1)
    tm = EVEN_TM

    @pl.when(si == 0)
    def _():
        state_ref[...] = jnp.zeros_like(state_ref)
        zbuf_ref[0:V7X_SUBLANES, :] = jnp.zeros((V7X_SUBLANES, CONV_DIM), F32)

    x = x_ref[0]
    h = _norm_mod(x, ng_ref[...], _mod_row(mod_ref, 1, b), _mod_row(mod_ref, 0, b)).astype(BF16)
    proj = jnp.dot(h, win_ref[...], preferred_element_type=F32)

    b_gate = proj[:, 0:CONV_DIM]
    z = proj[:, CONV_DIM:2 * CONV_DIM] * proj[:, 2 * CONV_DIM:3 * CONV_DIM]
    zbuf_ref[V7X_SUBLANES:V7X_SUBLANES + tm, :] = z
    z1 = zbuf_ref[V7X_SUBLANES - 1:V7X_SUBLANES - 1 + tm, :]
    z2 = zbuf_ref[V7X_SUBLANES - 2:V7X_SUBLANES - 2 + tm, :]
    cw = cw_ref[...]
    y = cw[0:1, :] * z2 + cw[1:2, :] * z1 + cw[2:3, :] * z
    mix_ref[0, :, 0:CONV_DIM] = (b_gate * y).astype(BF16)
    zbuf_ref[0:V7X_SUBLANES, :] = zbuf_ref[tm:tm + V7X_SUBLANES, :]

    q_off = 3 * CONV_DIM
    k_off = q_off + RET_DIM
    v_off = k_off + RET_DIM
    g_off = v_off + RET_DIM
    for r in range(tm // RET_TILE):
        r0 = r * RET_TILE
        cos = rot_ref[0, r0:r0 + RET_TILE, :]
        sin = rot_ref[1, r0:r0 + RET_TILE, :]
        for hh in range(RET_HEADS):
            c0 = hh * RET_HEAD_DIM
            q = proj[r0:r0 + RET_TILE, q_off + c0:q_off + c0 + RET_HEAD_DIM]
            k = proj[r0:r0 + RET_TILE, k_off + c0:k_off + c0 + RET_HEAD_DIM]
            v = proj[r0:r0 + RET_TILE, v_off + c0:v_off + c0 + RET_HEAD_DIM].astype(BF16)
            g = proj[r0:r0 + RET_TILE, g_off + c0:g_off + c0 + RET_HEAD_DIM]
            qr = q * cos + pltpu.roll(q, RET_HEAD_DIM // 2, axis=1) * sin
            kr = k * cos + pltpu.roll(k, RET_HEAD_DIM // 2, axis=1) * sin
            s = lax.dot_general(qr.astype(BF16), kr.astype(BF16), (((1,), (1,)), ((), ())),
                                preferred_element_type=F32)
            s = s * dec_ref[hh]
            o = jnp.dot(s.astype(BF16), v, preferred_element_type=F32)
            st = state_ref[hh]
            o = o + jnp.dot((qr * qdec_ref[hh]).astype(BF16), st.astype(BF16),
                            preferred_element_type=F32)
            kd = (kr * kdec_ref[hh]).astype(BF16)
            kv = lax.dot_general(kd, v, (((0,), (0,)), ((), ())), preferred_element_type=F32)
            state_ref[hh] = st * sdec_ref[hh, 0:1, :] + kv
            ms = jnp.mean(o * o, axis=-1, keepdims=True)
            rn = (o * lax.rsqrt(ms + EPS)) * rg_ref[:, c0:c0 + RET_HEAD_DIM]
            out = (g * _sigmoid(g)) * rn
            mix_ref[0, r0:r0 + RET_TILE, CONV_DIM + c0:CONV_DIM + c0 + RET_HEAD_DIM] = out.astype(BF16)


def _even_mix(x, mod, l, norm_g, w_in, conv_w, ret_g, tables):
    dec, qdec, kdec, sdec, rot = tables
    tm = EVEN_TM
    cw = jnp.pad(conv_w, ((0, V7X_SUBLANES - CONV_WIDTH), (0, 0)))
    const2 = lambda b, s: (0, 0)
    const3 = lambda b, s: (0, 0, 0)
    return pl.pallas_call(
        _even_kernel,
        grid=(BATCH, SEQ // tm),
        in_specs=[
            pl.BlockSpec((1, tm, D_MODEL), lambda b, s: (b, s, 0)),
            pl.BlockSpec((1, 6, V7X_SUBLANES, D_MODEL), lambda b, s: (l, 0, 0, 0)),
            pl.BlockSpec((1, D_MODEL), const2),
            pl.BlockSpec((D_MODEL, EVEN_IN), const2),
            pl.BlockSpec((V7X_SUBLANES, CONV_DIM), const2),
            pl.BlockSpec((1, RET_DIM), const2),
            pl.BlockSpec((2, tm, RET_HEAD_DIM), lambda b, s: (0, s, 0)),
            pl.BlockSpec((RET_HEADS, RET_TILE, RET_TILE), const3),
            pl.BlockSpec((RET_HEADS, RET_TILE, V7X_LANES), const3),
            pl.BlockSpec((RET_HEADS, RET_TILE, V7X_LANES), const3),
            pl.BlockSpec((RET_HEADS, V7X_SUBLANES, V7X_LANES), const3),
        ],
        out_specs=pl.BlockSpec((1, tm, D_MODEL), lambda b, s: (b, s, 0)),
        out_shape=jax.ShapeDtypeStruct((BATCH, SEQ, D_MODEL), BF16),
        scratch_shapes=[
            pltpu.VMEM((RET_HEADS, RET_HEAD_DIM, RET_HEAD_DIM), F32),
            pltpu.VMEM((tm + V7X_SUBLANES, CONV_DIM), F32),
        ],
        compiler_params=pltpu.CompilerParams(
            dimension_semantics=("arbitrary", "arbitrary"),
            vmem_limit_bytes=56 * 1024 * 1024),
        name="even_mix",
    )(x, mod, norm_g.reshape(1, D_MODEL), w_in, cw, ret_g.reshape(1, RET_DIM), rot, dec, qdec,
      kdec, sdec)


def _qkv_kernel(x_ref, mod_ref, ng_ref, w_ref, qg_ref, kg_ref, q_ref, k_ref, v_ref):
    b = pl.program_id(0)
    x = x_ref[0]
    h = _norm_mod(x, ng_ref[...], _mod_row(mod_ref, 1, b), _mod_row(mod_ref, 0, b)).astype(BF16)
    qkv = jnp.dot(h, w_ref[...], preferred_element_type=F32)
    qg = qg_ref[...] * (SB_HEAD_DIM ** -0.5 * LOG2E)
    kg = kg_ref[...]
    for hh in range(SB_HEADS):
        c0 = hh * SB_HEAD_DIM
        q = qkv[:, c0:c0 + SB_HEAD_DIM]
        k = qkv[:, SB_DIM + c0:SB_DIM + c0 + SB_HEAD_DIM]
        v = qkv[:, 2 * SB_DIM + c0:2 * SB_DIM + c0 + SB_HEAD_DIM]
        qn = (q * lax.rsqrt(jnp.mean(q * q, axis=-1, keepdims=True) + EPS)) * qg
        kn = (k * lax.rsqrt(jnp.mean(k * k, axis=-1, keepdims=True) + EPS)) * kg
        q_ref[0, hh] = qn.astype(BF16)
        k_ref[0, hh] = kn.astype(BF16)
        v_ref[0, hh] = v.astype(BF16)


def _qkv(x, mod, l, norm_g, w_qkv, q_g, k_g):
    tm = QKV_TM
    const2 = lambda b, s: (0, 0)
    head_spec = pl.BlockSpec((1, SB_HEADS, tm, SB_HEAD_DIM), lambda b, s: (b, 0, s, 0))
    head_shape = jax.ShapeDtypeStruct((BATCH, SB_HEADS, SEQ, SB_HEAD_DIM), BF16)
    return pl.pallas_call(
        _qkv_kernel,
        grid=(BATCH, SEQ // tm),
        in_specs=[
            pl.BlockSpec((1, tm, D_MODEL), lambda b, s: (b, s, 0)),
            pl.BlockSpec((1, 6, V7X_SUBLANES, D_MODEL), lambda b, s: (l, 0, 0, 0)),
            pl.BlockSpec((1, D_MODEL), const2),
            pl.BlockSpec((D_MODEL, 3 * SB_DIM), const2),
            pl.BlockSpec((1, SB_HEAD_DIM), const2),
            pl.BlockSpec((1, SB_HEAD_DIM), const2),
        ],
        out_specs=[head_spec, head_spec, head_spec],
        out_shape=[head_shape, head_shape, head_shape],
        compiler_params=pltpu.CompilerParams(
            dimension_semantics=("parallel", "parallel"),
            vmem_limit_bytes=56 * 1024 * 1024),
        name="qkv",
    )(x, mod, norm_g.reshape(1, D_MODEL), w_qkv, q_g.reshape(1, SB_HEAD_DIM),
      k_g.reshape(1, SB_HEAD_DIM))


def _softplus2(z):
    return jnp.maximum(z, 0.0) + jnp.log(1.0 + jnp.exp2(-jnp.abs(z))) * LOG2E


def _sb_kernel(q_ref, k_ref, v_ref, o_ref, acc_ref, r_ref, spb0, spb1, lb0, lb1, a0, a1):
    qi = pl.program_id(2)
    base = qi * SB_NSUB
    row = lax.broadcasted_iota(jnp.int32, (SB_TK, SB_TK), 0)
    col = lax.broadcasted_iota(jnp.int32, (SB_TK, SB_TK), 1)
    later = (row > col).astype(BF16)
    sets = ((spb0, lb0), (spb1, lb1))
    abuf = (a0, a1)

    acc_ref[...] = jnp.zeros_like(acc_ref)
    r_ref[...] = jnp.zeros_like(r_ref)

    def kblock(ref, j):
        return ref[0, 0, pl.ds(pl.multiple_of(j * SB_TK, SB_TK), SB_TK), :]

    def stage_a(j, i, mask_off=None):
        spb_ref, lb_ref = sets[i]
        z = lax.dot_general(q_ref[0, 0], kblock(k_ref, j), (((1,), (1,)), ((), ())),
                            preferred_element_type=F32)
        sp = _softplus2(z)
        lb = z - sp
        if mask_off is not None:
            qpos = lax.broadcasted_iota(jnp.int32, (SB_TQ, SB_TK), 0)
            kpos = lax.broadcasted_iota(jnp.int32, (SB_TQ, SB_TK), 1) + mask_off * SB_TK
            keep = kpos < qpos
            sp = jnp.where(keep, sp, 0.0)
            lb = jnp.where(keep, lb, MASKED_LOGIT)
        spb_ref[...] = sp.astype(BF16)
        lb_ref[...] = lb

    def stage_b(i):
        spb_ref, lb_ref = sets[i]
        cs = jnp.dot(spb_ref[...], later, preferred_element_type=F32)
        abuf[i][...] = (lb_ref[...] - cs) - r_ref[...]
        r_ref[...] += cs[:, 0:1] + spb_ref[:, 0:1].astype(F32)

    def stage_c(j, i):
        w = jnp.exp2(abuf[i][...]).astype(BF16)
        acc_ref[...] += jnp.dot(w, kblock(v_ref, j), preferred_element_type=F32)

    stage_a(base + 1, 0, mask_off=1)
    stage_a(base, 1, mask_off=0)
    stage_b(0)

    def body(p, carry):
        j = base - 1 - 2 * p
        stage_a(j, 0)
        stage_b(1)
        stage_c(j + 2, 0)
        stage_a(j - 1, 1)
        stage_b(0)
        stage_c(j + 1, 1)
        return carry

    lax.fori_loop(0, qi, body, 0)
    stage_b(1)
    stage_c(1, 0)
    stage_c(0, 1)
    o_ref[0] = acc_ref[...].astype(BF16)


def _sb_attention(q, k, v):
    kv_spec = pl.BlockSpec((1, 1, SEQ, SB_HEAD_DIM), lambda b, h, i: (b, h, 0, 0))
    return pl.pallas_call(
        _sb_kernel,
        grid=(BATCH, SB_HEADS, SEQ // SB_TQ),
        in_specs=[
            pl.BlockSpec((1, 1, SB_TQ, SB_HEAD_DIM), lambda b, h, i: (b, h, i, 0)),
            kv_spec, kv_spec,
        ],
        out_specs=pl.BlockSpec((1, SB_TQ, SB_HEAD_DIM), lambda b, h, i: (b, i, h)),
        out_shape=jax.ShapeDtypeStruct((BATCH, SEQ, SB_DIM), BF16),
        scratch_shapes=[
            pltpu.VMEM((SB_TQ, SB_HEAD_DIM), F32), pltpu.VMEM((SB_TQ, 1), F32),
            pltpu.VMEM((SB_TQ, SB_TK), BF16), pltpu.VMEM((SB_TQ, SB_TK), BF16),
            pltpu.VMEM((SB_TQ, SB_TK), F32), pltpu.VMEM((SB_TQ, SB_TK), F32),
            pltpu.VMEM((SB_TQ, SB_TK), F32), pltpu.VMEM((SB_TQ, SB_TK), F32),
        ],
        compiler_params=pltpu.CompilerParams(
            dimension_semantics=("parallel", "parallel", "parallel")),
        name="sb_attn",
    )(q, k, v)


def _post_kernel(x_ref, mix_ref, mod_ref, wo_ref, ng_ref, wg_ref, wu_ref, wd_ref, o_ref):
    b = pl.program_id(0)
    x = x_ref[0]
    y = jnp.dot(mix_ref[0], wo_ref[...], preferred_element_type=F32)
    x1 = x + _mod_row(mod_ref, 2, b) * y
    h = _norm_mod(x1, ng_ref[...], _mod_row(mod_ref, 4, b), _mod_row(mod_ref, 3, b)).astype(BF16)
    gate = jnp.dot(h, wg_ref[...], preferred_element_type=F32)
    up = jnp.dot(h, wu_ref[...], preferred_element_type=F32)
    act = ((gate * _sigmoid(gate)) * up).astype(BF16)
    ffn = jnp.dot(act, wd_ref[...], preferred_element_type=F32)
    o_ref[0] = x1 + _mod_row(mod_ref, 5, b) * ffn


def _post(x, mix, mod, l, w_out, norm_g, w_gate, w_up, w_down):
    tm = POST_TM
    const2 = lambda b, s: (0, 0)
    return pl.pallas_call(
        _post_kernel,
        grid=(BATCH, SEQ // tm),
        in_specs=[
            pl.BlockSpec((1, tm, D_MODEL), lambda b, s: (b, s, 0)),
            pl.BlockSpec((1, tm, D_MODEL), lambda b, s: (b, s, 0)),
            pl.BlockSpec((1, 6, V7X_SUBLANES, D_MODEL), lambda b, s: (l, 0, 0, 0)),
            pl.BlockSpec((D_MODEL, D_MODEL), const2),
            pl.BlockSpec((1, D_MODEL), const2),
            pl.BlockSpec((D_MODEL, D_FF), const2),
            pl.BlockSpec((D_MODEL, D_FF), const2),
            pl.BlockSpec((D_FF, D_MODEL), const2),
        ],
        out_specs=pl.BlockSpec((1, tm, D_MODEL), lambda b, s: (b, s, 0)),
        out_shape=jax.ShapeDtypeStruct((BATCH, SEQ, D_MODEL), F32),
        compiler_params=pltpu.CompilerParams(
            dimension_semantics=("parallel", "parallel"),
            vmem_limit_bytes=58 * 1024 * 1024),
        name="post",
    )(x, mix, mod, w_out, norm_g.reshape(1, D_MODEL), w_gate, w_up, w_down)


def kernel(x, c, ada_w, ada_b, norm_mix_g, norm_ffn_g, ev_w_in, ev_conv_w, ev_ret_norm_g, ev_w_out,
           od_w_qkv, od_q_norm_g, od_k_norm_g, od_w_out, ffn_w_gate, ffn_w_up, ffn_w_down):
    bf = lambda w: w.astype(BF16)
    ev_w_in, ev_w_out, od_w_qkv, od_w_out = bf(ev_w_in), bf(ev_w_out), bf(od_w_qkv), bf(od_w_out)
    ffn_w_gate, ffn_w_up, ffn_w_down = bf(ffn_w_gate), bf(ffn_w_up), bf(ffn_w_down)
    mod = _adaln(c, ada_w, ada_b)
    tables = _retention_tables()
    for l in range(DEPTH):
        j = l // 2
        if l % 2 == 0:
            mix = _even_mix(x, mod, l, norm_mix_g[l], ev_w_in[j], ev_conv_w[j], ev_ret_norm_g[j],
                            tables)
            w_out = ev_w_out[j]
        else:
            q, k, v = _qkv(x, mod, l, norm_mix_g[l], od_w_qkv[j], od_q_norm_g[j], od_k_norm_g[j])
            mix = _sb_attention(q, k, v)
            w_out = od_w_out[j]
        x = _post(x, mix, mod, l, w_out, norm_ffn_g[l], ffn_w_gate[l], ffn_w_up[l], ffn_w_down[l])
    return x
```

```python
import functools
import math

import jax
import jax.numpy as jnp
import numpy as np
from jax import lax
from jax.experimental import pallas as pl
from jax.experimental.pallas import tpu as pltpu

D_MODEL = 1024
BATCH = 4
SEQ = 4096
DEPTH = 4
CHUNK = 64
EPS = 1e-6
CONV_WIDTH = 3
CONV_DIM = 512
RET_HEADS = 4
RET_HEAD_DIM = 128
RET_DIM = 512
ROPE_THETA = 10000.0
SB_HEADS = 8
SB_HEAD_DIM = 128
SB_DIM = 1024
D_FF = 2816
EVEN_IN = 3 * CONV_DIM + 4 * RET_DIM

V7X_SUBLANES = 8
V7X_LANES = 128
V7X_MXU_DIM = 256
V7X_VMEM_BYTES = 64 * 1024 * 1024

LOG2E = 1.0 / math.log(2.0)

EVEN_TM = 512
RET_TILE = V7X_MXU_DIM
QKV_TM = 512
POST_TM = 256
SB_TK = V7X_MXU_DIM
SB_NSUB = 2
SB_TQ = SB_NSUB * SB_TK
MASKED_LOGIT = -1e30
F32_MIN_SUBNORMAL_LOG2 = -149.0
SB_EXIT_LOG2 = 160.0
assert SB_EXIT_LOG2 > -F32_MIN_SUBNORMAL_LOG2 + 1

F32 = jnp.float32
BF16 = jnp.bfloat16


def _sigmoid(x):
    return 1.0 / (1.0 + jnp.exp(-x))


def _norm_mod(x, gain, scale, shift):
    ms = jnp.mean(x * x, axis=-1, keepdims=True)
    return (x * lax.rsqrt(ms + EPS)) * (gain * (1.0 + scale)) + shift


def _mod_row(mod_ref, k, b):
    return mod_ref[0, k, pl.ds(b, 1), :]


def _adaln_kernel(c_ref, w_ref, b_ref, o_ref):
    c = c_ref[...]
    ca = c * _sigmoid(c)
    o_ref[0, 0] = jnp.dot(ca, w_ref[0], precision=lax.Precision.HIGHEST,
                          preferred_element_type=F32) + b_ref[0, 0]


def _adaln(c, ada_w, ada_b):
    c_pad = jnp.pad(c, ((0, V7X_SUBLANES - BATCH), (0, 0)))
    b4 = ada_b.reshape(DEPTH, 6, 1, D_MODEL)
    return pl.pallas_call(
        _adaln_kernel,
        grid=(DEPTH, 6),
        in_specs=[
            pl.BlockSpec((V7X_SUBLANES, D_MODEL), lambda l, k: (0, 0)),
            pl.BlockSpec((1, D_MODEL, D_MODEL), lambda l, k: (l, 0, k)),
            pl.BlockSpec((1, 1, 1, D_MODEL), lambda l, k: (l, k, 0, 0)),
        ],
        out_specs=pl.BlockSpec((1, 1, V7X_SUBLANES, D_MODEL), lambda l, k: (l, k, 0, 0)),
        out_shape=jax.ShapeDtypeStruct((DEPTH, 6, V7X_SUBLANES, D_MODEL), F32),
        compiler_params=pltpu.CompilerParams(dimension_semantics=("parallel", "parallel")),
        name="adaln",
    )(c_pad, ada_w, b4)


def _retention_tables():
    h = np.arange(RET_HEADS, dtype=np.float64)
    log_g = np.log1p(-np.exp2(-5.0 - h))
    idx = np.arange(RET_TILE, dtype=np.float64)
    diff = idx[:, None] - idx[None, :]
    same_or_earlier = (idx[None, :] // CHUNK) <= (idx[:, None] // CHUNK)
    k_scale = RET_HEAD_DIM ** -0.5
    dec = np.exp(np.abs(diff)[None] * log_g[:, None, None]) * same_or_earlier[None] * k_scale
    qdec = np.exp((idx + 1.0)[None, :] * log_g[:, None])
    kdec = np.exp((RET_TILE - 1.0 - idx)[None, :] * log_g[:, None]) * k_scale
    sdec = np.exp(RET_TILE * log_g)
    qdec = np.broadcast_to(qdec[:, :, None], (RET_HEADS, RET_TILE, V7X_LANES))
    kdec = np.broadcast_to(kdec[:, :, None], (RET_HEADS, RET_TILE, V7X_LANES))
    sdec = np.broadcast_to(sdec[:, None, None], (RET_HEADS, V7X_SUBLANES, V7X_LANES))
    inv_freq = 1.0 / (ROPE_THETA ** (np.arange(0, RET_HEAD_DIM, 2, dtype=np.float64) / RET_HEAD_DIM))
    ang = np.arange(SEQ, dtype=np.float64)[:, None] * inv_freq[None, :]
    cos2 = np.concatenate([np.cos(ang), np.cos(ang)], axis=-1)
    sin2 = np.concatenate([-np.sin(ang), np.sin(ang)], axis=-1)
    rot = np.stack([cos2, sin2])
    f = lambda a: jnp.asarray(np.ascontiguousarray(a), dtype=F32)
    return f(dec), f(qdec), f(kdec), f(sdec), f(rot)


def _even_kernel(x_ref, mod_ref, ng_ref, win_ref, cw_ref, rg_ref, rot_ref, dec_ref, qdec_ref,
                 kdec_ref, sdec_ref, mix_ref, state_ref, zbuf_ref):
    b = pl.program_id(0)
    si = pl.program_id(1)
    tm = EVEN_TM

    @pl.when(si == 0)
    def _():
        state_ref[...] = jnp.zeros_like(state_ref)
        zbuf_ref[0:V7X_SUBLANES, :] = jnp.zeros((V7X_SUBLANES, CONV_DIM), F32)

    x = x_ref[0]
    h = _norm_mod(x, ng_ref[...], _mod_row(mod_ref, 1, b), _mod_row(mod_ref, 0, b)).astype(BF16)
    proj = jnp.dot(h, win_ref[...], preferred_element_type=F32)

    b_gate = proj[:, 0:CONV_DIM]
    z = proj[:, CONV_DIM:2 * CONV_DIM] * proj[:, 2 * CONV_DIM:3 * CONV_DIM]
    zbuf_ref[V7X_SUBLANES:V7X_SUBLANES + tm, :] = z
    z1 = zbuf_ref[V7X_SUBLANES - 1:V7X_SUBLANES - 1 + tm, :]
    z2 = zbuf_ref[V7X_SUBLANES - 2:V7X_SUBLANES - 2 + tm, :]
    cw = cw_ref[...]
    y = cw[0:1, :] * z2 + cw[1:2, :] * z1 + cw[2:3, :] * z
    mix_ref[0, :, 0:CONV_DIM] = (b_gate * y).astype(BF16)
    zbuf_ref[0:V7X_SUBLANES, :] = zbuf_ref[tm:tm + V7X_SUBLANES, :]

    q_off = 3 * CONV_DIM
    k_off = q_off + RET_DIM
    v_off = k_off + RET_DIM
    g_off = v_off + RET_DIM
    for r in range(tm // RET_TILE):
        r0 = r * RET_TILE
        cos = rot_ref[0, r0:r0 + RET_TILE, :]
        sin = rot_ref[1, r0:r0 + RET_TILE, :]
        for hh in range(RET_HEADS):
            c0 = hh * RET_HEAD_DIM
            q = proj[r0:r0 + RET_TILE, q_off + c0:q_off + c0 + RET_HEAD_DIM]
            k = proj[r0:r0 + RET_TILE, k_off + c0:k_off + c0 + RET_HEAD_DIM]
            v = proj[r0:r0 + RET_TILE, v_off + c0:v_off + c0 + RET_HEAD_DIM].astype(BF16)
            g = proj[r0:r0 + RET_TILE, g_off + c0:g_off + c0 + RET_HEAD_DIM]
            qr = q * cos + pltpu.roll(q, RET_HEAD_DIM // 2, axis=1) * sin
            kr = k * cos + pltpu.roll(k, RET_HEAD_DIM // 2, axis=1) * sin
            s = lax.dot_general(qr.astype(BF16), kr.astype(BF16), (((1,), (1,)), ((), ())),
                                preferred_element_type=F32)
            s = s * dec_ref[hh]
            o = jnp.dot(s.astype(BF16), v, preferred_element_type=F32)
            st = state_ref[hh]
            o = o + jnp.dot((qr * qdec_ref[hh]).astype(BF16), st.astype(BF16),
                            preferred_element_type=F32)
            kd = (kr * kdec_ref[hh]).astype(BF16)
            kv = lax.dot_general(kd, v, (((0,), (0,)), ((), ())), preferred_element_type=F32)
            state_ref[hh] = st * sdec_ref[hh, 0:1, :] + kv
            ms = jnp.mean(o * o, axis=-1, keepdims=True)
            rn = (o * lax.rsqrt(ms + EPS)) * rg_ref[:, c0:c0 + RET_HEAD_DIM]
            out = (g * _sigmoid(g)) * rn
            mix_ref[0, r0:r0 + RET_TILE, CONV_DIM + c0:CONV_DIM + c0 + RET_HEAD_DIM] = out.astype(BF16)


def _even_mix(x, mod, l, norm_g, w_in, conv_w, ret_g, tables):
    dec, qdec, kdec, sdec, rot = tables
    tm = EVEN_TM
    cw = jnp.pad(conv_w, ((0, V7X_SUBLANES - CONV_WIDTH), (0, 0)))
    const2 = lambda b, s: (0, 0)
    const3 = lambda b, s: (0, 0, 0)
    return pl.pallas_call(
        _even_kernel,
        grid=(BATCH, SEQ // tm),
        in_specs=[
            pl.BlockSpec((1, tm, D_MODEL), lambda b, s: (b, s, 0)),
            pl.BlockSpec((1, 6, V7X_SUBLANES, D_MODEL), lambda b, s: (l, 0, 0, 0)),
            pl.BlockSpec((1, D_MODEL), const2),
            pl.BlockSpec((D_MODEL, EVEN_IN), const2),
            pl.BlockSpec((V7X_SUBLANES, CONV_DIM), const2),
            pl.BlockSpec((1, RET_DIM), const2),
            pl.BlockSpec((2, tm, RET_HEAD_DIM), lambda b, s: (0, s, 0)),
            pl.BlockSpec((RET_HEADS, RET_TILE, RET_TILE), const3),
            pl.BlockSpec((RET_HEADS, RET_TILE, V7X_LANES), const3),
            pl.BlockSpec((RET_HEADS, RET_TILE, V7X_LANES), const3),
            pl.BlockSpec((RET_HEADS, V7X_SUBLANES, V7X_LANES), const3),
        ],
        out_specs=pl.BlockSpec((1, tm, D_MODEL), lambda b, s: (b, s, 0)),
        out_shape=jax.ShapeDtypeStruct((BATCH, SEQ, D_MODEL), BF16),
        scratch_shapes=[
            pltpu.VMEM((RET_HEADS, RET_HEAD_DIM, RET_HEAD_DIM), F32),
            pltpu.VMEM((tm + V7X_SUBLANES, CONV_DIM), F32),
        ],
        compiler_params=pltpu.CompilerParams(
            dimension_semantics=("arbitrary", "arbitrary"),
            vmem_limit_bytes=56 * 1024 * 1024),
        name="even_mix",
    )(x, mod, norm_g.reshape(1, D_MODEL), w_in, cw, ret_g.reshape(1, RET_DIM), rot, dec, qdec,
      kdec, sdec)


def _qkv_kernel(x_ref, mod_ref, ng_ref, w_ref, qg_ref, kg_ref, q_ref, k_ref, v_ref):
    b = pl.program_id(0)
    x = x_ref[0]
    h = _norm_mod(x, ng_ref[...], _mod_row(mod_ref, 1, b), _mod_row(mod_ref, 0, b)).astype(BF16)
    qkv = jnp.dot(h, w_ref[...], preferred_element_type=F32)
    qg = qg_ref[...] * (SB_HEAD_DIM ** -0.5 * LOG2E)
    kg = kg_ref[...]
    for hh in range(SB_HEADS):
        c0 = hh * SB_HEAD_DIM
        q = qkv[:, c0:c0 + SB_HEAD_DIM]
        k = qkv[:, SB_DIM + c0:SB_DIM + c0 + SB_HEAD_DIM]
        v = qkv[:, 2 * SB_DIM + c0:2 * SB_DIM + c0 + SB_HEAD_DIM]
        qn = (q * lax.rsqrt(jnp.mean(q * q, axis=-1, keepdims=True) + EPS)) * qg
        kn = (k * lax.rsqrt(jnp.mean(k * k, axis=-1, keepdims=True) + EPS)) * kg
        q_ref[0, hh] = qn.astype(BF16)
        k_ref[0, hh] = kn.astype(BF16)
        v_ref[0, hh] = v.astype(BF16)


def _qkv(x, mod, l, norm_g, w_qkv, q_g, k_g):
    tm = QKV_TM
    const2 = lambda b, s: (0, 0)
    head_spec = pl.BlockSpec((1, SB_HEADS, tm, SB_HEAD_DIM), lambda b, s: (b, 0, s, 0))
    head_shape = jax.ShapeDtypeStruct((BATCH, SB_HEADS, SEQ, SB_HEAD_DIM), BF16)
    return pl.pallas_call(
        _qkv_kernel,
        grid=(BATCH, SEQ // tm),
        in_specs=[
            pl.BlockSpec((1, tm, D_MODEL), lambda b, s: (b, s, 0)),
            pl.BlockSpec((1, 6, V7X_SUBLANES, D_MODEL), lambda b, s: (l, 0, 0, 0)),
            pl.BlockSpec((1, D_MODEL), const2),
            pl.BlockSpec((D_MODEL, 3 * SB_DIM), const2),
            pl.BlockSpec((1, SB_HEAD_DIM), const2),
            pl.BlockSpec((1, SB_HEAD_DIM), const2),
        ],
        out_specs=[head_spec, head_spec, head_spec],
        out_shape=[head_shape, head_shape, head_shape],
        compiler_params=pltpu.CompilerParams(
            dimension_semantics=("parallel", "parallel"),
            vmem_limit_bytes=56 * 1024 * 1024),
        name="qkv",
    )(x, mod, norm_g.reshape(1, D_MODEL), w_qkv, q_g.reshape(1, SB_HEAD_DIM),
      k_g.reshape(1, SB_HEAD_DIM))


def _softplus2(z):
    return jnp.maximum(z, 0.0) + jnp.log(1.0 + jnp.exp2(-jnp.abs(z))) * LOG2E


def _sb_kernel(q_ref, k_ref, v_ref, o_ref, acc_ref, r_ref, spb0, spb1, lb0, lb1, a0, a1):
    qi = pl.program_id(2)
    base = qi * SB_NSUB
    row = lax.broadcasted_iota(jnp.int32, (SB_TK, SB_TK), 0)
    col = lax.broadcasted_iota(jnp.int32, (SB_TK, SB_TK), 1)
    later = (row > col).astype(BF16)
    sets = ((spb0, lb0), (spb1, lb1))
    abuf = (a0, a1)

    acc_ref[...] = jnp.zeros_like(acc_ref)
    r_ref[...] = jnp.zeros_like(r_ref)

    def kblock(ref, j):
        return ref[0, 0, pl.ds(pl.multiple_of(j * SB_TK, SB_TK), SB_TK), :]

    def stage_a(j, i, mask_off=None):
        spb_ref, lb_ref = sets[i]
        z = lax.dot_general(q_ref[0, 0], kblock(k_ref, j), (((1,), (1,)), ((), ())),
                            preferred_element_type=F32)
        sp = _softplus2(z)
        lb = z - sp
        if mask_off is not None:
            qpos = lax.broadcasted_iota(jnp.int32, (SB_TQ, SB_TK), 0)
            kpos = lax.broadcasted_iota(jnp.int32, (SB_TQ, SB_TK), 1) + mask_off * SB_TK
            keep = kpos < qpos
            sp = jnp.where(keep, sp, 0.0)
            lb = jnp.where(keep, lb, MASKED_LOGIT)
        spb_ref[...] = sp.astype(BF16)
        lb_ref[...] = lb

    def stage_b(i):
        spb_ref, lb_ref = sets[i]
        cs = jnp.dot(spb_ref[...], later, preferred_element_type=F32)
        abuf[i][...] = (lb_ref[...] - cs) - r_ref[...]
        r_ref[...] += cs[:, 0:1] + spb_ref[:, 0:1].astype(F32)

    def stage_c(j, i):
        w = jnp.exp2(abuf[i][...]).astype(BF16)
        acc_ref[...] += jnp.dot(w, kblock(v_ref, j), preferred_element_type=F32)

    stage_a(base + 1, 0, mask_off=1)
    stage_a(base, 1, mask_off=0)
    stage_b(0)

    def body(carry):
        p, _ = carry
        j = base - 1 - 2 * p
        stage_a(j, 0)
        stage_b(1)
        stage_c(j + 2, 0)
        stage_a(j - 1, 1)
        stage_b(0)
        stage_c(j + 1, 1)
        return p + 1, jnp.min(r_ref[...])

    def cond(carry):
        p, r_min = carry
        return jnp.logical_and(p < qi, r_min <= SB_EXIT_LOG2)

    p_end, _ = lax.while_loop(cond, body, (jnp.int32(0), jnp.float32(0.0)))
    j_last = base - 2 * p_end
    stage_b(1)
    stage_c(j_last + 1, 0)
    stage_c(j_last, 1)
    o_ref[0] = acc_ref[...].astype(BF16)


def _sb_attention(q, k, v):
    kv_spec = pl.BlockSpec((1, 1, SEQ, SB_HEAD_DIM), lambda b, h, i: (b, h, 0, 0))
    return pl.pallas_call(
        _sb_kernel,
        grid=(BATCH, SB_HEADS, SEQ // SB_TQ),
        in_specs=[
            pl.BlockSpec((1, 1, SB_TQ, SB_HEAD_DIM), lambda b, h, i: (b, h, i, 0)),
            kv_spec, kv_spec,
        ],
        out_specs=pl.BlockSpec((1, SB_TQ, SB_HEAD_DIM), lambda b, h, i: (b, i, h)),
        out_shape=jax.ShapeDtypeStruct((BATCH, SEQ, SB_DIM), BF16),
        scratch_shapes=[
            pltpu.VMEM((SB_TQ, SB_HEAD_DIM), F32), pltpu.VMEM((SB_TQ, 1), F32),
            pltpu.VMEM((SB_TQ, SB_TK), BF16), pltpu.VMEM((SB_TQ, SB_TK), BF16),
            pltpu.VMEM((SB_TQ, SB_TK), F32), pltpu.VMEM((SB_TQ, SB_TK), F32),
            pltpu.VMEM((SB_TQ, SB_TK), F32), pltpu.VMEM((SB_TQ, SB_TK), F32),
        ],
        compiler_params=pltpu.CompilerParams(
            dimension_semantics=("parallel", "parallel", "parallel")),
        name="sb_attn",
    )(q, k, v)


def _post_kernel(x_ref, mix_ref, mod_ref, wo_ref, ng_ref, wg_ref, wu_ref, wd_ref, o_ref):
    b = pl.program_id(0)
    x = x_ref[0]
    y = jnp.dot(mix_ref[0], wo_ref[...], preferred_element_type=F32)
    x1 = x + _mod_row(mod_ref, 2, b) * y
    h = _norm_mod(x1, ng_ref[...], _mod_row(mod_ref, 4, b), _mod_row(mod_ref, 3, b)).astype(BF16)
    gate = jnp.dot(h, wg_ref[...], preferred_element_type=F32)
    up = jnp.dot(h, wu_ref[...], preferred_element_type=F32)
    act = ((gate * _sigmoid(gate)) * up).astype(BF16)
    ffn = jnp.dot(act, wd_ref[...], preferred_element_type=F32)
    o_ref[0] = x1 + _mod_row(mod_ref, 5, b) * ffn


def _post(x, mix, mod, l, w_out, norm_g, w_gate, w_up, w_down):
    tm = POST_TM
    const2 = lambda b, s: (0, 0)
    return pl.pallas_call(
        _post_kernel,
        grid=(BATCH, SEQ // tm),
        in_specs=[
            pl.BlockSpec((1, tm, D_MODEL), lambda b, s: (b, s, 0)),
            pl.BlockSpec((1, tm, D_MODEL), lambda b, s: (b, s, 0)),
            pl.BlockSpec((1, 6, V7X_SUBLANES, D_MODEL), lambda b, s: (l, 0, 0, 0)),
            pl.BlockSpec((D_MODEL, D_MODEL), const2),
            pl.BlockSpec((1, D_MODEL), const2),
            pl.BlockSpec((D_MODEL, D_FF), const2),
            pl.BlockSpec((D_MODEL, D_FF), const2),
            pl.BlockSpec((D_FF, D_MODEL), const2),
        ],
        out_specs=pl.BlockSpec((1, tm, D_MODEL), lambda b, s: (b, s, 0)),
        out_shape=jax.ShapeDtypeStruct((BATCH, SEQ, D_MODEL), F32),
        compiler_params=pltpu.CompilerParams(
            dimension_semantics=("parallel", "parallel"),
            vmem_limit_bytes=58 * 1024 * 1024),
        name="post",
    )(x, mix, mod, w_out, norm_g.reshape(1, D_MODEL), w_gate, w_up, w_down)


def kernel(x, c, ada_w, ada_b, norm_mix_g, norm_ffn_g, ev_w_in, ev_conv_w, ev_ret_norm_g, ev_w_out,
           od_w_qkv, od_q_norm_g, od_k_norm_g, od_w_out, ffn_w_gate, ffn_w_up, ffn_w_down):
    bf = lambda w: w.astype(BF16)
    ev_w_in, ev_w_out, od_w_qkv, od_w_out = bf(ev_w_in), bf(ev_w_out), bf(od_w_qkv), bf(od_w_out)
    ffn_w_gate, ffn_w_up, ffn_w_down = bf(ffn_w_gate), bf(ffn_w_up), bf(ffn_w_down)
    mod = _adaln(c, ada_w, ada_b)
    tables = _retention_tables()
    for l in range(DEPTH):
        j = l // 2
        if l % 2 == 0:
            mix = _even_mix(x, mod, l, norm_mix_g[l], ev_w_in[j], ev_conv_w[j], ev_ret_norm_g[j],
                            tables)
            w_out = ev_w_out[j]
        else:
            q, k, v = _qkv(x, mod, l, norm_mix_g[l], od_w_qkv[j], od_q_norm_g[j], od_k_norm_g[j])
            mix = _sb_attention(q, k, v)
            w_out = od_w_out[j]
        x = _post(x, mix, mod, l, w_out, norm_ffn_g[l], ffn_w_gate[l], ffn_w_up[l], ffn_w_down[l])
    return x
```

```python
import functools
import math

import jax
import jax.numpy as jnp
import numpy as np
from jax import lax
from jax.experimental import pallas as pl
from jax.experimental.pallas import tpu as pltpu

D_MODEL = 1024
BATCH = 4
SEQ = 4096
DEPTH = 4
CHUNK = 64
EPS = 1e-6
CONV_WIDTH = 3
CONV_DIM = 512
RET_HEADS = 4
RET_HEAD_DIM = 128
RET_DIM = 512
ROPE_THETA = 10000.0
SB_HEADS = 8
SB_HEAD_DIM = 128
SB_DIM = 1024
D_FF = 2816
EVEN_IN = 3 * CONV_DIM + 4 * RET_DIM

V7X_SUBLANES = 8
V7X_LANES = 128
V7X_MXU_DIM = 256
V7X_VMEM_BYTES = 64 * 1024 * 1024

LOG2E = 1.0 / math.log(2.0)

EVEN_TM = 512
RET_TILE = V7X_MXU_DIM
QKV_TM = 512
POST_TM = 256
SB_TK = V7X_MXU_DIM
MASKED_LOGIT = -1e30
F32_MIN_SUBNORMAL_LOG2 = -149.0
SB_EXIT_LOG2 = 160.0
assert SB_EXIT_LOG2 > -F32_MIN_SUBNORMAL_LOG2 + 1

F32 = jnp.float32
BF16 = jnp.bfloat16


def _sigmoid(x):
    return 1.0 / (1.0 + jnp.exp(-x))


def _norm_mod(x, gain, scale, shift):
    ms = jnp.mean(x * x, axis=-1, keepdims=True)
    return (x * lax.rsqrt(ms + EPS)) * (gain * (1.0 + scale)) + shift


def _mod_row(mod_ref, k, b):
    return mod_ref[0, k, pl.ds(b, 1), :]


def _adaln_kernel(c_ref, w_ref, b_ref, o_ref):
    c = c_ref[...]
    ca = c * _sigmoid(c)
    o_ref[0, 0] = jnp.dot(ca, w_ref[0], precision=lax.Precision.HIGHEST,
                          preferred_element_type=F32) + b_ref[0, 0]


def _adaln(c, ada_w, ada_b):
    c_pad = jnp.pad(c, ((0, V7X_SUBLANES - BATCH), (0, 0)))
    b4 = ada_b.reshape(DEPTH, 6, 1, D_MODEL)
    return pl.pallas_call(
        _adaln_kernel,
        grid=(DEPTH, 6),
        in_specs=[
            pl.BlockSpec((V7X_SUBLANES, D_MODEL), lambda l, k: (0, 0)),
            pl.BlockSpec((1, D_MODEL, D_MODEL), lambda l, k: (l, 0, k)),
            pl.BlockSpec((1, 1, 1, D_MODEL), lambda l, k: (l, k, 0, 0)),
        ],
        out_specs=pl.BlockSpec((1, 1, V7X_SUBLANES, D_MODEL), lambda l, k: (l, k, 0, 0)),
        out_shape=jax.ShapeDtypeStruct((DEPTH, 6, V7X_SUBLANES, D_MODEL), F32),
        compiler_params=pltpu.CompilerParams(dimension_semantics=("parallel", "parallel")),
        name="adaln",
    )(c_pad, ada_w, b4)


def _retention_tables():
    h = np.arange(RET_HEADS, dtype=np.float64)
    log_g = np.log1p(-np.exp2(-5.0 - h))
    idx = np.arange(RET_TILE, dtype=np.float64)
    diff = idx[:, None] - idx[None, :]
    same_or_earlier = (idx[None, :] // CHUNK) <= (idx[:, None] // CHUNK)
    k_scale = RET_HEAD_DIM ** -0.5
    dec = np.exp(np.abs(diff)[None] * log_g[:, None, None]) * same_or_earlier[None] * k_scale
    qdec = np.exp((idx + 1.0)[None, :] * log_g[:, None])
    kdec = np.exp((RET_TILE - 1.0 - idx)[None, :] * log_g[:, None]) * k_scale
    sdec = np.exp(RET_TILE * log_g)
    qdec = np.broadcast_to(qdec[:, :, None], (RET_HEADS, RET_TILE, V7X_LANES))
    kdec = np.broadcast_to(kdec[:, :, None], (RET_HEADS, RET_TILE, V7X_LANES))
    sdec = np.broadcast_to(sdec[:, None, None], (RET_HEADS, V7X_SUBLANES, V7X_LANES))
    inv_freq = 1.0 / (ROPE_THETA ** (np.arange(0, RET_HEAD_DIM, 2, dtype=np.float64) / RET_HEAD_DIM))
    ang = np.arange(SEQ, dtype=np.float64)[:, None] * inv_freq[None, :]
    cos2 = np.concatenate([np.cos(ang), np.cos(ang)], axis=-1)
    sin2 = np.concatenate([-np.sin(ang), np.sin(ang)], axis=-1)
    rot = np.stack([cos2, sin2])
    f = lambda a: jnp.asarray(np.ascontiguousarray(a), dtype=F32)
    return f(dec), f(qdec), f(kdec), f(sdec), f(rot)


def _even_kernel(x_ref, mod_ref, ng_ref, win_ref, cw_ref, rg_ref, rot_ref, dec_ref, qdec_ref,
                 kdec_ref, sdec_ref, mix_ref, state_ref, zbuf_ref):
    b = pl.program_id(0)
    si = pl.program_id(1)
    tm = EVEN_TM

    @pl.when(si == 0)
    def _():
        state_ref[...] = jnp.zeros_like(state_ref)
        zbuf_ref[0:V7X_SUBLANES, :] = jnp.zeros((V7X_SUBLANES, CONV_DIM), F32)

    x = x_ref[0]
    h = _norm_mod(x, ng_ref[...], _mod_row(mod_ref, 1, b), _mod_row(mod_ref, 0, b)).astype(BF16)
    proj = jnp.dot(h, win_ref[...], preferred_element_type=F32)

    b_gate = proj[:, 0:CONV_DIM]
    z = proj[:, CONV_DIM:2 * CONV_DIM] * proj[:, 2 * CONV_DIM:3 * CONV_DIM]
    zbuf_ref[V7X_SUBLANES:V7X_SUBLANES + tm, :] = z
    z1 = zbuf_ref[V7X_SUBLANES - 1:V7X_SUBLANES - 1 + tm, :]
    z2 = zbuf_ref[V7X_SUBLANES - 2:V7X_SUBLANES - 2 + tm, :]
    cw = cw_ref[...]
    y = cw[0:1, :] * z2 + cw[1:2, :] * z1 + cw[2:3, :] * z
    mix_ref[0, :, 0:CONV_DIM] = (b_gate * y).astype(BF16)
    zbuf_ref[0:V7X_SUBLANES, :] = zbuf_ref[tm:tm + V7X_SUBLANES, :]

    q_off = 3 * CONV_DIM
    k_off = q_off + RET_DIM
    v_off = k_off + RET_DIM
    g_off = v_off + RET_DIM
    for r in range(tm // RET_TILE):
        r0 = r * RET_TILE
        cos = rot_ref[0, r0:r0 + RET_TILE, :]
        sin = rot_ref[1, r0:r0 + RET_TILE, :]
        for hh in range(RET_HEADS):
            c0 = hh * RET_HEAD_DIM
            q = proj[r0:r0 + RET_TILE, q_off + c0:q_off + c0 + RET_HEAD_DIM]
            k = proj[r0:r0 + RET_TILE, k_off + c0:k_off + c0 + RET_HEAD_DIM]
            v = proj[r0:r0 + RET_TILE, v_off + c0:v_off + c0 + RET_HEAD_DIM].astype(BF16)
            g = proj[r0:r0 + RET_TILE, g_off + c0:g_off + c0 + RET_HEAD_DIM]
            qr = q * cos + pltpu.roll(q, RET_HEAD_DIM // 2, axis=1) * sin
            kr = k * cos + pltpu.roll(k, RET_HEAD_DIM // 2, axis=1) * sin
            s = lax.dot_general(qr.astype(BF16), kr.astype(BF16), (((1,), (1,)), ((), ())),
                                preferred_element_type=F32)
            s = s * dec_ref[hh]
            o = jnp.dot(s.astype(BF16), v, preferred_element_type=F32)
            st = state_ref[hh]
            o = o + jnp.dot((qr * qdec_ref[hh]).astype(BF16), st.astype(BF16),
                            preferred_element_type=F32)
            kd = (kr * kdec_ref[hh]).astype(BF16)
            kv = lax.dot_general(kd, v, (((0,), (0,)), ((), ())), preferred_element_type=F32)
            state_ref[hh] = st * sdec_ref[hh, 0:1, :] + kv
            ms = jnp.mean(o * o, axis=-1, keepdims=True)
            rn = (o * lax.rsqrt(ms + EPS)) * rg_ref[:, c0:c0 + RET_HEAD_DIM]
            out = (g * _sigmoid(g)) * rn
            mix_ref[0, r0:r0 + RET_TILE, CONV_DIM + c0:CONV_DIM + c0 + RET_HEAD_DIM] = out.astype(BF16)


def _even_mix(x, mod, l, norm_g, w_in, conv_w, ret_g, tables):
    dec, qdec, kdec, sdec, rot = tables
    tm = EVEN_TM
    cw = jnp.pad(conv_w, ((0, V7X_SUBLANES - CONV_WIDTH), (0, 0)))
    const2 = lambda b, s: (0, 0)
    const3 = lambda b, s: (0, 0, 0)
    return pl.pallas_call(
        _even_kernel,
        grid=(BATCH, SEQ // tm),
        in_specs=[
            pl.BlockSpec((1, tm, D_MODEL), lambda b, s: (b, s, 0)),
            pl.BlockSpec((1, 6, V7X_SUBLANES, D_MODEL), lambda b, s: (l, 0, 0, 0)),
            pl.BlockSpec((1, D_MODEL), const2),
            pl.BlockSpec((D_MODEL, EVEN_IN), const2),
            pl.BlockSpec((V7X_SUBLANES, CONV_DIM), const2),
            pl.BlockSpec((1, RET_DIM), const2),
            pl.BlockSpec((2, tm, RET_HEAD_DIM), lambda b, s: (0, s, 0)),
            pl.BlockSpec((RET_HEADS, RET_TILE, RET_TILE), const3),
            pl.BlockSpec((RET_HEADS, RET_TILE, V7X_LANES), const3),
            pl.BlockSpec((RET_HEADS, RET_TILE, V7X_LANES), const3),
            pl.BlockSpec((RET_HEADS, V7X_SUBLANES, V7X_LANES), const3),
        ],
        out_specs=pl.BlockSpec((1, tm, D_MODEL), lambda b, s: (b, s, 0)),
        out_shape=jax.ShapeDtypeStruct((BATCH, SEQ, D_MODEL), BF16),
        scratch_shapes=[
            pltpu.VMEM((RET_HEADS, RET_HEAD_DIM, RET_HEAD_DIM), F32),
            pltpu.VMEM((tm + V7X_SUBLANES, CONV_DIM), F32),
        ],
        compiler_params=pltpu.CompilerParams(
            dimension_semantics=("arbitrary", "arbitrary"),
            vmem_limit_bytes=56 * 1024 * 1024),
        name="even_mix",
    )(x, mod, norm_g.reshape(1, D_MODEL), w_in, cw, ret_g.reshape(1, RET_DIM), rot, dec, qdec,
      kdec, sdec)


def _qkv_kernel(x_ref, mod_ref, ng_ref, w_ref, qg_ref, kg_ref, q_ref, k_ref, v_ref):
    b = pl.program_id(0)
    x = x_ref[0]
    h = _norm_mod(x, ng_ref[...], _mod_row(mod_ref, 1, b), _mod_row(mod_ref, 0, b)).astype(BF16)
    qkv = jnp.dot(h, w_ref[...], preferred_element_type=F32)
    qg = qg_ref[...] * (SB_HEAD_DIM ** -0.5 * LOG2E)
    kg = kg_ref[...]
    for hh in range(SB_HEADS):
        c0 = hh * SB_HEAD_DIM
        q = qkv[:, c0:c0 + SB_HEAD_DIM]
        k = qkv[:, SB_DIM + c0:SB_DIM + c0 + SB_HEAD_DIM]
        v = qkv[:, 2 * SB_DIM + c0:2 * SB_DIM + c0 + SB_HEAD_DIM]
        qn = (q * lax.rsqrt(jnp.mean(q * q, axis=-1, keepdims=True) + EPS)) * qg
        kn = (k * lax.rsqrt(jnp.mean(k * k, axis=-1, keepdims=True) + EPS)) * kg
        q_ref[0, hh] = qn.astype(BF16)
        k_ref[0, hh] = kn.astype(BF16)
        v_ref[0, hh] = v.astype(BF16)


def _qkv(x, mod, l, norm_g, w_qkv, q_g, k_g):
    tm = QKV_TM
    const2 = lambda b, s: (0, 0)
    head_spec = pl.BlockSpec((1, SB_HEADS, tm, SB_HEAD_DIM), lambda b, s: (b, 0, s, 0))
    head_shape = jax.ShapeDtypeStruct((BATCH, SB_HEADS, SEQ, SB_HEAD_DIM), BF16)
    return pl.pallas_call(
        _qkv_kernel,
        grid=(BATCH, SEQ // tm),
        in_specs=[
            pl.BlockSpec((1, tm, D_MODEL), lambda b, s: (b, s, 0)),
            pl.BlockSpec((1, 6, V7X_SUBLANES, D_MODEL), lambda b, s: (l, 0, 0, 0)),
            pl.BlockSpec((1, D_MODEL), const2),
            pl.BlockSpec((D_MODEL, 3 * SB_DIM), const2),
            pl.BlockSpec((1, SB_HEAD_DIM), const2),
            pl.BlockSpec((1, SB_HEAD_DIM), const2),
        ],
        out_specs=[head_spec, head_spec, head_spec],
        out_shape=[head_shape, head_shape, head_shape],
        compiler_params=pltpu.CompilerParams(
            dimension_semantics=("parallel", "parallel"),
            vmem_limit_bytes=56 * 1024 * 1024),
        name="qkv",
    )(x, mod, norm_g.reshape(1, D_MODEL), w_qkv, q_g.reshape(1, SB_HEAD_DIM),
      k_g.reshape(1, SB_HEAD_DIM))


def _softplus2(z):
    return jnp.maximum(z, 0.0) + jnp.log(1.0 + jnp.exp2(-jnp.abs(z))) * LOG2E


def _sb_kernel(q_ref, k_ref, v_ref, o_ref, spb0, spb1, lb0, lb1, a0, a1, rc_ref, ac_ref,
               acc_all, r_all):
    row = lax.broadcasted_iota(jnp.int32, (SB_TK, SB_TK), 0)
    col = lax.broadcasted_iota(jnp.int32, (SB_TK, SB_TK), 1)
    later = (row > col).astype(BF16)
    causal = col < row
    sets = ((spb0, lb0), (spb1, lb1))
    abuf = (a0, a1)
    nblk = SEQ // SB_TK

    def blk(ref, j, rows=SB_TK):
        return ref[0, 0, pl.ds(pl.multiple_of(j * SB_TK, SB_TK), rows), :]

    def qk(q, j):
        return lax.dot_general(q, blk(k_ref, j), (((1,), (1,)), ((), ())),
                               preferred_element_type=F32)

    def stage_a(j, i, m):
        spb_ref, lb_ref = sets[i]
        z = qk(blk(q_ref, j, m), j)
        sp = _softplus2(z)
        lb = z - sp
        spb_ref[0:SB_TK] = jnp.where(causal, sp[:SB_TK], 0.0).astype(BF16)
        lb_ref[0:SB_TK] = jnp.where(causal, lb[:SB_TK], MASKED_LOGIT)
        if m > SB_TK:
            spb_ref[SB_TK:m] = sp[SB_TK:].astype(BF16)
            lb_ref[SB_TK:m] = lb[SB_TK:]

    def stage_b(j, i, m):
        spb_ref, lb_ref = sets[i]
        cs = jnp.dot(spb_ref[0:m], later, preferred_element_type=F32)
        rowsum = cs[:, 0:1] + spb_ref[0:m, 0:1].astype(F32)
        abuf[i][0:SB_TK] = lb_ref[0:SB_TK] - cs[:SB_TK]
        if m > SB_TK:
            r_old = rc_ref[...]
            abuf[i][SB_TK:m] = (lb_ref[SB_TK:m] - cs[SB_TK:]) - r_old
            r_all[pl.ds(pl.multiple_of((j + 1) * SB_TK, SB_TK), SB_TK), :] = r_old + rowsum[SB_TK:]
        rc_ref[...] = rowsum[:SB_TK]

    def stage_c(j, i, m):
        w = jnp.exp2(abuf[i][0:m]).astype(BF16)
        pv = jnp.dot(w, blk(v_ref, j), preferred_element_type=F32)
        if m > SB_TK:
            acc_all[pl.ds(pl.multiple_of((j + 1) * SB_TK, SB_TK), SB_TK), :] = ac_ref[...] + pv[SB_TK:]
        ac_ref[...] = pv[:SB_TK]

    def tick(tau, par, a=True, b=True, c=True):
        rows = lambda t: SB_TK if (isinstance(t, int) and t == 0) else 2 * SB_TK
        if a:
            stage_a(nblk - 1 - tau, par, rows(tau))
        if b:
            stage_b(nblk - tau, 1 - par, rows(tau - 1))
        if c:
            stage_c(nblk + 1 - tau, par, rows(tau - 2))

    tick(0, 0, b=False, c=False)
    tick(1, 1, c=False)
    tick(2, 0)

    def body(p, carry):
        tau = 3 + 2 * p
        tick(tau, 1)
        tick(tau + 1, 0)
        return carry

    lax.fori_loop(0, (nblk - 4) // 2, body, 0)
    tick(nblk - 1, 1)
    tick(nblk, 0, a=False)
    tick(nblk + 1, 1, a=False, b=False)
    acc_all[0:SB_TK] = ac_ref[...]

    r_all[0:2 * SB_TK] = jnp.full((2 * SB_TK, 1), 2 * SB_EXIT_LOG2, F32)

    @pl.when(jnp.min(r_all[...]) <= SB_EXIT_LOG2)
    def _():
        def per_block(m, carry):
            rows = pl.ds(pl.multiple_of(m * SB_TK, SB_TK), SB_TK)

            def step(c2):
                j, _ = c2
                z = qk(blk(q_ref, m), j)
                sp = _softplus2(z)
                spb = sp.astype(BF16)
                cs = jnp.dot(spb, later, preferred_element_type=F32)
                r = r_all[rows, :]
                w = jnp.exp2(((z - sp) - cs) - r).astype(BF16)
                acc_all[rows, :] += jnp.dot(w, blk(v_ref, j), preferred_element_type=F32)
                r = r + (cs[:, 0:1] + spb[:, 0:1].astype(F32))
                r_all[rows, :] = r
                return j - 1, jnp.min(r)

            def more(c2):
                j, r_min = c2
                return jnp.logical_and(j >= 0, r_min <= SB_EXIT_LOG2)

            lax.while_loop(more, step, (m - 2, jnp.min(r_all[rows, :])))
            return carry

        lax.fori_loop(2, nblk, per_block, 0)

    o_ref[0] = acc_all[...].astype(BF16)


def _sb_attention(q, k, v):
    head_spec = pl.BlockSpec((1, 1, SEQ, SB_HEAD_DIM), lambda b, h: (b, h, 0, 0))
    return pl.pallas_call(
        _sb_kernel,
        grid=(BATCH, SB_HEADS),
        in_specs=[head_spec, head_spec, head_spec],
        out_specs=pl.BlockSpec((1, SEQ, SB_HEAD_DIM), lambda b, h: (b, 0, h)),
        out_shape=jax.ShapeDtypeStruct((BATCH, SEQ, SB_DIM), BF16),
        scratch_shapes=[
            pltpu.VMEM((2 * SB_TK, SB_TK), BF16), pltpu.VMEM((2 * SB_TK, SB_TK), BF16),
            pltpu.VMEM((2 * SB_TK, SB_TK), F32), pltpu.VMEM((2 * SB_TK, SB_TK), F32),
            pltpu.VMEM((2 * SB_TK, SB_TK), F32), pltpu.VMEM((2 * SB_TK, SB_TK), F32),
            pltpu.VMEM((SB_TK, 1), F32), pltpu.VMEM((SB_TK, SB_HEAD_DIM), F32),
            pltpu.VMEM((SEQ, SB_HEAD_DIM), F32), pltpu.VMEM((SEQ, 1), F32),
        ],
        compiler_params=pltpu.CompilerParams(dimension_semantics=("parallel", "parallel")),
        name="sb_attn",
    )(q, k, v)


def _post_kernel(x_ref, mix_ref, mod_ref, wo_ref, ng_ref, wg_ref, wu_ref, wd_ref, o_ref):
    b = pl.program_id(0)
    x = x_ref[0]
    y = jnp.dot(mix_ref[0], wo_ref[...], preferred_element_type=F32)
    x1 = x + _mod_row(mod_ref, 2, b) * y
    h = _norm_mod(x1, ng_ref[...], _mod_row(mod_ref, 4, b), _mod_row(mod_ref, 3, b)).astype(BF16)
    gate = jnp.dot(h, wg_ref[...], preferred_element_type=F32)
    up = jnp.dot(h, wu_ref[...], preferred_element_type=F32)
    act = ((gate * _sigmoid(gate)) * up).astype(BF16)
    ffn = jnp.dot(act, wd_ref[...], preferred_element_type=F32)
    o_ref[0] = x1 + _mod_row(mod_ref, 5, b) * ffn


def _post(x, mix, mod, l, w_out, norm_g, w_gate, w_up, w_down):
    tm = POST_TM
    const2 = lambda b, s: (0, 0)
    return pl.pallas_call(
        _post_kernel,
        grid=(BATCH, SEQ // tm),
        in_specs=[
            pl.BlockSpec((1, tm, D_MODEL), lambda b, s: (b, s, 0)),
            pl.BlockSpec((1, tm, D_MODEL), lambda b, s: (b, s, 0)),
            pl.BlockSpec((1, 6, V7X_SUBLANES, D_MODEL), lambda b, s: (l, 0, 0, 0)),
            pl.BlockSpec((D_MODEL, D_MODEL), const2),
            pl.BlockSpec((1, D_MODEL), const2),
            pl.BlockSpec((D_MODEL, D_FF), const2),
            pl.BlockSpec((D_MODEL, D_FF), const2),
            pl.BlockSpec((D_FF, D_MODEL), const2),
        ],
        out_specs=pl.BlockSpec((1, tm, D_MODEL), lambda b, s: (b, s, 0)),
        out_shape=jax.ShapeDtypeStruct((BATCH, SEQ, D_MODEL), F32),
        compiler_params=pltpu.CompilerParams(
            dimension_semantics=("parallel", "parallel"),
            vmem_limit_bytes=58 * 1024 * 1024),
        name="post",
    )(x, mix, mod, w_out, norm_g.reshape(1, D_MODEL), w_gate, w_up, w_down)


def kernel(x, c, ada_w, ada_b, norm_mix_g, norm_ffn_g, ev_w_in, ev_conv_w, ev_ret_norm_g, ev_w_out,
           od_w_qkv, od_q_norm_g, od_k_norm_g, od_w_out, ffn_w_gate, ffn_w_up, ffn_w_down):
    bf = lambda w: w.astype(BF16)
    mod = _adaln(c, ada_w, ada_b)
    tables = _retention_tables()
    for l in range(DEPTH):
        j = l // 2
        if l % 2 == 0:
            mix = _even_mix(x, mod, l, norm_mix_g[l], bf(ev_w_in[j]), ev_conv_w[j],
                            ev_ret_norm_g[j], tables)
            w_out = ev_w_out[j]
        else:
            q, k, v = _qkv(x, mod, l, norm_mix_g[l], bf(od_w_qkv[j]), od_q_norm_g[j],
                           od_k_norm_g[j])
            mix = _sb_attention(q, k, v)
            w_out = od_w_out[j]
        x = _post(x, mix, mod, l, bf(w_out), norm_ffn_g[l], bf(ffn_w_gate[l]), bf(ffn_w_up[l]),
                  bf(ffn_w_down[l]))
    return x
```

```python
import functools
import math

import jax
import jax.numpy as jnp
import numpy as np
from jax import lax
from jax.experimental import pallas as pl
from jax.experimental.pallas import tpu as pltpu

D_MODEL = 1024
BATCH = 4
SEQ = 4096
DEPTH = 4
CHUNK = 64
EPS = 1e-6
CONV_WIDTH = 3
CONV_DIM = 512
RET_HEADS = 4
RET_HEAD_DIM = 128
RET_DIM = 512
ROPE_THETA = 10000.0
SB_HEADS = 8
SB_HEAD_DIM = 128
SB_DIM = 1024
D_FF = 2816
EVEN_IN = 3 * CONV_DIM + 4 * RET_DIM

V7X_SUBLANES = 8
V7X_LANES = 128
V7X_MXU_DIM = 256
V7X_VMEM_BYTES = 64 * 1024 * 1024

LOG2E = 1.0 / math.log(2.0)

EVEN_TM = 512
RET_TILE = V7X_MXU_DIM
QKV_TM = 512
POST_TM = 512
SB_TK = V7X_MXU_DIM
MASKED_LOGIT = -1e30
F32_MIN_SUBNORMAL_LOG2 = -149.0
SB_EXIT_LOG2 = 160.0
assert SB_EXIT_LOG2 > -F32_MIN_SUBNORMAL_LOG2 + 1

F32 = jnp.float32
BF16 = jnp.bfloat16


def _sigmoid(x):
    return 1.0 / (1.0 + jnp.exp(-x))


def _norm_mod(x, gain, scale, shift):
    ms = jnp.mean(x * x, axis=-1, keepdims=True)
    return (x * lax.rsqrt(ms + EPS)) * (gain * (1.0 + scale)) + shift


def _mod_row(mod_ref, k, b):
    return mod_ref[0, k, pl.ds(b, 1), :]


def _adaln_kernel(c_ref, w_ref, b_ref, o_ref):
    c = c_ref[...]
    ca = c * _sigmoid(c)
    o_ref[0, 0] = jnp.dot(ca, w_ref[0], precision=lax.Precision.HIGHEST,
                          preferred_element_type=F32) + b_ref[0, 0]


def _adaln(c, ada_w, ada_b):
    c_pad = jnp.pad(c, ((0, V7X_SUBLANES - BATCH), (0, 0)))
    b4 = ada_b.reshape(DEPTH, 6, 1, D_MODEL)
    return pl.pallas_call(
        _adaln_kernel,
        grid=(DEPTH, 6),
        in_specs=[
            pl.BlockSpec((V7X_SUBLANES, D_MODEL), lambda l, k: (0, 0)),
            pl.BlockSpec((1, D_MODEL, D_MODEL), lambda l, k: (l, 0, k)),
            pl.BlockSpec((1, 1, 1, D_MODEL), lambda l, k: (l, k, 0, 0)),
        ],
        out_specs=pl.BlockSpec((1, 1, V7X_SUBLANES, D_MODEL), lambda l, k: (l, k, 0, 0)),
        out_shape=jax.ShapeDtypeStruct((DEPTH, 6, V7X_SUBLANES, D_MODEL), F32),
        compiler_params=pltpu.CompilerParams(dimension_semantics=("parallel", "parallel")),
        name="adaln",
    )(c_pad, ada_w, b4)


def _retention_tables():
    h = np.arange(RET_HEADS, dtype=np.float64)
    log_g = np.log1p(-np.exp2(-5.0 - h))
    idx = np.arange(RET_TILE, dtype=np.float64)
    diff = idx[:, None] - idx[None, :]
    same_or_earlier = (idx[None, :] // CHUNK) <= (idx[:, None] // CHUNK)
    k_scale = RET_HEAD_DIM ** -0.5
    dec = np.exp(np.abs(diff)[None] * log_g[:, None, None]) * same_or_earlier[None] * k_scale
    qdec = np.exp((idx + 1.0)[None, :] * log_g[:, None])
    kdec = np.exp((RET_TILE - 1.0 - idx)[None, :] * log_g[:, None]) * k_scale
    sdec = np.exp(RET_TILE * log_g)
    qdec = np.broadcast_to(qdec[:, :, None], (RET_HEADS, RET_TILE, V7X_LANES))
    kdec = np.broadcast_to(kdec[:, :, None], (RET_HEADS, RET_TILE, V7X_LANES))
    sdec = np.broadcast_to(sdec[:, None, None], (RET_HEADS, V7X_SUBLANES, V7X_LANES))
    inv_freq = 1.0 / (ROPE_THETA ** (np.arange(0, RET_HEAD_DIM, 2, dtype=np.float64) / RET_HEAD_DIM))
    ang = np.arange(SEQ, dtype=np.float64)[:, None] * inv_freq[None, :]
    cos2 = np.concatenate([np.cos(ang), np.cos(ang)], axis=-1)
    sin2 = np.concatenate([-np.sin(ang), np.sin(ang)], axis=-1)
    rot = np.stack([cos2, sin2])
    f = lambda a: jnp.asarray(np.ascontiguousarray(a), dtype=F32)
    return f(dec), f(qdec), f(kdec), f(sdec), f(rot)


def _even_kernel(x_ref, mod_ref, ng_ref, win_ref, cw_ref, rg_ref, rot_ref, dec_ref, qdec_ref,
                 kdec_ref, sdec_ref, mix_ref, state_ref, zbuf_ref):
    b = pl.program_id(0)
    si = pl.program_id(1)
    tm = EVEN_TM

    @pl.when(si == 0)
    def _():
        state_ref[...] = jnp.zeros_like(state_ref)
        zbuf_ref[0:V7X_SUBLANES, :] = jnp.zeros((V7X_SUBLANES, CONV_DIM), F32)

    x = x_ref[0]
    h = _norm_mod(x, ng_ref[0], _mod_row(mod_ref, 1, b), _mod_row(mod_ref, 0, b)).astype(BF16)
    proj = jnp.dot(h, win_ref[0], preferred_element_type=F32)

    b_gate = proj[:, 0:CONV_DIM]
    z = proj[:, CONV_DIM:2 * CONV_DIM] * proj[:, 2 * CONV_DIM:3 * CONV_DIM]
    zbuf_ref[V7X_SUBLANES:V7X_SUBLANES + tm, :] = z
    z1 = zbuf_ref[V7X_SUBLANES - 1:V7X_SUBLANES - 1 + tm, :]
    z2 = zbuf_ref[V7X_SUBLANES - 2:V7X_SUBLANES - 2 + tm, :]
    cw = cw_ref[0]
    y = cw[0:1, :] * z2 + cw[1:2, :] * z1 + cw[2:3, :] * z
    mix_ref[0, :, 0:CONV_DIM] = (b_gate * y).astype(BF16)
    zbuf_ref[0:V7X_SUBLANES, :] = zbuf_ref[tm:tm + V7X_SUBLANES, :]

    q_off = 3 * CONV_DIM
    k_off = q_off + RET_DIM
    v_off = k_off + RET_DIM
    g_off = v_off + RET_DIM
    for r in range(tm // RET_TILE):
        r0 = r * RET_TILE
        cos = rot_ref[0, r0:r0 + RET_TILE, :]
        sin = rot_ref[1, r0:r0 + RET_TILE, :]
        for hh in range(RET_HEADS):
            c0 = hh * RET_HEAD_DIM
            q = proj[r0:r0 + RET_TILE, q_off + c0:q_off + c0 + RET_HEAD_DIM]
            k = proj[r0:r0 + RET_TILE, k_off + c0:k_off + c0 + RET_HEAD_DIM]
            v = proj[r0:r0 + RET_TILE, v_off + c0:v_off + c0 + RET_HEAD_DIM].astype(BF16)
            g = proj[r0:r0 + RET_TILE, g_off + c0:g_off + c0 + RET_HEAD_DIM]
            qr = q * cos + pltpu.roll(q, RET_HEAD_DIM // 2, axis=1) * sin
            kr = k * cos + pltpu.roll(k, RET_HEAD_DIM // 2, axis=1) * sin
            s = lax.dot_general(qr.astype(BF16), kr.astype(BF16), (((1,), (1,)), ((), ())),
                                preferred_element_type=F32)
            s = s * dec_ref[hh]
            o = jnp.dot(s.astype(BF16), v, preferred_element_type=F32)
            st = state_ref[hh]
            o = o + jnp.dot((qr * qdec_ref[hh]).astype(BF16), st.astype(BF16),
                            preferred_element_type=F32)
            kd = (kr * kdec_ref[hh]).astype(BF16)
            kv = lax.dot_general(kd, v, (((0,), (0,)), ((), ())), preferred_element_type=F32)
            state_ref[hh] = st * sdec_ref[hh, 0:1, :] + kv
            ms = jnp.mean(o * o, axis=-1, keepdims=True)
            rn = (o * lax.rsqrt(ms + EPS)) * rg_ref[0, :, c0:c0 + RET_HEAD_DIM]
            out = (g * _sigmoid(g)) * rn
            mix_ref[0, r0:r0 + RET_TILE, CONV_DIM + c0:CONV_DIM + c0 + RET_HEAD_DIM] = out.astype(BF16)


def _even_mix(x, mod, l, j, norm_g, w_in, conv_w, ret_g, tables):
    dec, qdec, kdec, sdec, rot = tables
    tm = EVEN_TM
    const3 = lambda b, s: (0, 0, 0)
    layer = lambda b, s: (l, 0, 0)
    even = lambda b, s: (j, 0, 0)
    return pl.pallas_call(
        _even_kernel,
        grid=(BATCH, SEQ // tm),
        in_specs=[
            pl.BlockSpec((1, tm, D_MODEL), lambda b, s: (b, s, 0)),
            pl.BlockSpec((1, 6, V7X_SUBLANES, D_MODEL), lambda b, s: (l, 0, 0, 0)),
            pl.BlockSpec((1, 1, D_MODEL), layer),
            pl.BlockSpec((1, D_MODEL, EVEN_IN), even, pipeline_mode=pl.Buffered(1)),
            pl.BlockSpec((1, V7X_SUBLANES, CONV_DIM), even),
            pl.BlockSpec((1, 1, RET_DIM), even),
            pl.BlockSpec((2, tm, RET_HEAD_DIM), lambda b, s: (0, s, 0)),
            pl.BlockSpec((RET_HEADS, RET_TILE, RET_TILE), const3),
            pl.BlockSpec((RET_HEADS, RET_TILE, V7X_LANES), const3),
            pl.BlockSpec((RET_HEADS, RET_TILE, V7X_LANES), const3),
            pl.BlockSpec((RET_HEADS, V7X_SUBLANES, V7X_LANES), const3),
        ],
        out_specs=pl.BlockSpec((1, tm, D_MODEL), lambda b, s: (b, s, 0)),
        out_shape=jax.ShapeDtypeStruct((BATCH, SEQ, D_MODEL), BF16),
        scratch_shapes=[
            pltpu.VMEM((RET_HEADS, RET_HEAD_DIM, RET_HEAD_DIM), F32),
            pltpu.VMEM((tm + V7X_SUBLANES, CONV_DIM), F32),
        ],
        compiler_params=pltpu.CompilerParams(
            dimension_semantics=("arbitrary", "arbitrary"),
            vmem_limit_bytes=56 * 1024 * 1024),
        name="even_mix",
    )(x, mod, norm_g, w_in, conv_w, ret_g, rot, dec, qdec, kdec, sdec)


def _qkv_kernel(x_ref, mod_ref, ng_ref, w_ref, qg_ref, kg_ref, q_ref, k_ref, v_ref):
    b = pl.program_id(0)
    x = x_ref[0]
    h = _norm_mod(x, ng_ref[0], _mod_row(mod_ref, 1, b), _mod_row(mod_ref, 0, b)).astype(BF16)
    qkv = jnp.dot(h, w_ref[0], preferred_element_type=F32)
    qg = qg_ref[0] * (SB_HEAD_DIM ** -0.5 * LOG2E)
    kg = kg_ref[0]
    for hh in range(SB_HEADS):
        c0 = hh * SB_HEAD_DIM
        q = qkv[:, c0:c0 + SB_HEAD_DIM]
        k = qkv[:, SB_DIM + c0:SB_DIM + c0 + SB_HEAD_DIM]
        v = qkv[:, 2 * SB_DIM + c0:2 * SB_DIM + c0 + SB_HEAD_DIM]
        qn = (q * lax.rsqrt(jnp.mean(q * q, axis=-1, keepdims=True) + EPS)) * qg
        kn = (k * lax.rsqrt(jnp.mean(k * k, axis=-1, keepdims=True) + EPS)) * kg
        q_ref[0, hh] = qn.astype(BF16)
        k_ref[0, hh] = kn.astype(BF16)
        v_ref[0, hh] = v.astype(BF16)


def _qkv(x, mod, l, j, norm_g, w_qkv, q_g, k_g):
    tm = QKV_TM
    layer = lambda b, s: (l, 0, 0)
    odd = lambda b, s: (j, 0, 0)
    head_spec = pl.BlockSpec((1, SB_HEADS, tm, SB_HEAD_DIM), lambda b, s: (b, 0, s, 0))
    head_shape = jax.ShapeDtypeStruct((BATCH, SB_HEADS, SEQ, SB_HEAD_DIM), BF16)
    return pl.pallas_call(
        _qkv_kernel,
        grid=(BATCH, SEQ // tm),
        in_specs=[
            pl.BlockSpec((1, tm, D_MODEL), lambda b, s: (b, s, 0)),
            pl.BlockSpec((1, 6, V7X_SUBLANES, D_MODEL), lambda b, s: (l, 0, 0, 0)),
            pl.BlockSpec((1, 1, D_MODEL), layer),
            pl.BlockSpec((1, D_MODEL, 3 * SB_DIM), odd, pipeline_mode=pl.Buffered(1)),
            pl.BlockSpec((1, 1, SB_HEAD_DIM), odd),
            pl.BlockSpec((1, 1, SB_HEAD_DIM), odd),
        ],
        out_specs=[head_spec, head_spec, head_spec],
        out_shape=[head_shape, head_shape, head_shape],
        compiler_params=pltpu.CompilerParams(
            dimension_semantics=("parallel", "parallel"),
            vmem_limit_bytes=56 * 1024 * 1024),
        name="qkv",
    )(x, mod, norm_g, w_qkv, q_g, k_g)


def _softplus2(z):
    return jnp.maximum(z, 0.0) + jnp.log(1.0 + jnp.exp2(-jnp.abs(z))) * LOG2E


def _sb_kernel(q_ref, k_ref, v_ref, o_ref, spb0, spb1, lb0, lb1, a0, a1, rc_ref, ac_ref,
               acc_all, r_all):
    row = lax.broadcasted_iota(jnp.int32, (SB_TK, SB_TK), 0)
    col = lax.broadcasted_iota(jnp.int32, (SB_TK, SB_TK), 1)
    later = (row > col).astype(BF16)
    causal = col < row
    sets = ((spb0, lb0), (spb1, lb1))
    abuf = (a0, a1)
    nblk = SEQ // SB_TK

    def blk(ref, j, rows=SB_TK):
        return ref[0, 0, pl.ds(pl.multiple_of(j * SB_TK, SB_TK), rows), :]

    def qk(q, j):
        return lax.dot_general(q, blk(k_ref, j), (((1,), (1,)), ((), ())),
                               preferred_element_type=F32)

    def stage_a(j, i, m):
        spb_ref, lb_ref = sets[i]
        z = qk(blk(q_ref, j, m), j)
        sp = _softplus2(z)
        lb = z - sp
        spb_ref[0:SB_TK] = jnp.where(causal, sp[:SB_TK], 0.0).astype(BF16)
        lb_ref[0:SB_TK] = jnp.where(causal, lb[:SB_TK], MASKED_LOGIT)
        if m > SB_TK:
            spb_ref[SB_TK:m] = sp[SB_TK:].astype(BF16)
            lb_ref[SB_TK:m] = lb[SB_TK:]

    def stage_b(j, i, m):
        spb_ref, lb_ref = sets[i]
        cs = jnp.dot(spb_ref[0:m], later, preferred_element_type=F32)
        rowsum = cs[:, 0:1] + spb_ref[0:m, 0:1].astype(F32)
        abuf[i][0:SB_TK] = lb_ref[0:SB_TK] - cs[:SB_TK]
        if m > SB_TK:
            r_old = rc_ref[...]
            abuf[i][SB_TK:m] = (lb_ref[SB_TK:m] - cs[SB_TK:]) - r_old
            r_all[pl.ds(pl.multiple_of((j + 1) * SB_TK, SB_TK), SB_TK), :] = r_old + rowsum[SB_TK:]
        rc_ref[...] = rowsum[:SB_TK]

    def stage_c(j, i, m):
        w = jnp.exp2(abuf[i][0:m]).astype(BF16)
        pv = jnp.dot(w, blk(v_ref, j), preferred_element_type=F32)
        if m > SB_TK:
            acc_all[pl.ds(pl.multiple_of((j + 1) * SB_TK, SB_TK), SB_TK), :] = ac_ref[...] + pv[SB_TK:]
        ac_ref[...] = pv[:SB_TK]

    def tick(tau, par, a=True, b=True, c=True):
        rows = lambda t: SB_TK if (isinstance(t, int) and t == 0) else 2 * SB_TK
        if a:
            stage_a(nblk - 1 - tau, par, rows(tau))
        if b:
            stage_b(nblk - tau, 1 - par, rows(tau - 1))
        if c:
            stage_c(nblk + 1 - tau, par, rows(tau - 2))

    tick(0, 0, b=False, c=False)
    tick(1, 1, c=False)
    tick(2, 0)

    def body(p, carry):
        tau = 3 + 2 * p
        tick(tau, 1)
        tick(tau + 1, 0)
        return carry

    lax.fori_loop(0, (nblk - 4) // 2, body, 0)
    tick(nblk - 1, 1)
    tick(nblk, 0, a=False)
    tick(nblk + 1, 1, a=False, b=False)
    acc_all[0:SB_TK] = ac_ref[...]

    r_all[0:2 * SB_TK] = jnp.full((2 * SB_TK, 1), 2 * SB_EXIT_LOG2, F32)

    @pl.when(jnp.min(r_all[...]) <= SB_EXIT_LOG2)
    def _():
        def per_block(m, carry):
            rows = pl.ds(pl.multiple_of(m * SB_TK, SB_TK), SB_TK)

            def step(c2):
                j, _ = c2
                z = qk(blk(q_ref, m), j)
                sp = _softplus2(z)
                spb = sp.astype(BF16)
                cs = jnp.dot(spb, later, preferred_element_type=F32)
                r = r_all[rows, :]
                w = jnp.exp2(((z - sp) - cs) - r).astype(BF16)
                acc_all[rows, :] += jnp.dot(w, blk(v_ref, j), preferred_element_type=F32)
                r = r + (cs[:, 0:1] + spb[:, 0:1].astype(F32))
                r_all[rows, :] = r
                return j - 1, jnp.min(r)

            def more(c2):
                j, r_min = c2
                return jnp.logical_and(j >= 0, r_min <= SB_EXIT_LOG2)

            lax.while_loop(more, step, (m - 2, jnp.min(r_all[rows, :])))
            return carry

        lax.fori_loop(2, nblk, per_block, 0)

    o_ref[0] = acc_all[...].astype(BF16)


def _sb_attention(q, k, v):
    head_spec = pl.BlockSpec((1, 1, SEQ, SB_HEAD_DIM), lambda b, h: (b, h, 0, 0))
    return pl.pallas_call(
        _sb_kernel,
        grid=(BATCH, SB_HEADS),
        in_specs=[head_spec, head_spec, head_spec],
        out_specs=pl.BlockSpec((1, SEQ, SB_HEAD_DIM), lambda b, h: (b, 0, h)),
        out_shape=jax.ShapeDtypeStruct((BATCH, SEQ, SB_DIM), BF16),
        scratch_shapes=[
            pltpu.VMEM((2 * SB_TK, SB_TK), BF16), pltpu.VMEM((2 * SB_TK, SB_TK), BF16),
            pltpu.VMEM((2 * SB_TK, SB_TK), F32), pltpu.VMEM((2 * SB_TK, SB_TK), F32),
            pltpu.VMEM((2 * SB_TK, SB_TK), F32), pltpu.VMEM((2 * SB_TK, SB_TK), F32),
            pltpu.VMEM((SB_TK, 1), F32), pltpu.VMEM((SB_TK, SB_HEAD_DIM), F32),
            pltpu.VMEM((SEQ, SB_HEAD_DIM), F32), pltpu.VMEM((SEQ, 1), F32),
        ],
        compiler_params=pltpu.CompilerParams(dimension_semantics=("parallel", "parallel")),
        name="sb_attn",
    )(q, k, v)


def _post_kernel(x_ref, mix_ref, mod_ref, wo_ref, ng_ref, wg_ref, wu_ref, wd_ref, o_ref):
    b = pl.program_id(0)
    x = x_ref[0]
    y = jnp.dot(mix_ref[0], wo_ref[0], preferred_element_type=F32)
    x1 = x + _mod_row(mod_ref, 2, b) * y
    h = _norm_mod(x1, ng_ref[0], _mod_row(mod_ref, 4, b), _mod_row(mod_ref, 3, b)).astype(BF16)
    gate = jnp.dot(h, wg_ref[0], preferred_element_type=F32)
    up = jnp.dot(h, wu_ref[0], preferred_element_type=F32)
    act = ((gate * _sigmoid(gate)) * up).astype(BF16)
    ffn = jnp.dot(act, wd_ref[0], preferred_element_type=F32)
    o_ref[0] = x1 + _mod_row(mod_ref, 5, b) * ffn


def _post(x, mix, mod, l, j, w_out, norm_g, w_gate, w_up, w_down):
    tm = POST_TM
    layer = lambda b, s: (l, 0, 0)
    single = pl.Buffered(1)
    return pl.pallas_call(
        _post_kernel,
        grid=(BATCH, SEQ // tm),
        in_specs=[
            pl.BlockSpec((1, tm, D_MODEL), lambda b, s: (b, s, 0)),
            pl.BlockSpec((1, tm, D_MODEL), lambda b, s: (b, s, 0)),
            pl.BlockSpec((1, 6, V7X_SUBLANES, D_MODEL), lambda b, s: (l, 0, 0, 0)),
            pl.BlockSpec((1, D_MODEL, D_MODEL), lambda b, s: (j, 0, 0), pipeline_mode=single),
            pl.BlockSpec((1, 1, D_MODEL), layer),
            pl.BlockSpec((1, D_MODEL, D_FF), layer, pipeline_mode=single),
            pl.BlockSpec((1, D_MODEL, D_FF), layer, pipeline_mode=single),
            pl.BlockSpec((1, D_FF, D_MODEL), layer, pipeline_mode=single),
        ],
        out_specs=pl.BlockSpec((1, tm, D_MODEL), lambda b, s: (b, s, 0)),
        out_shape=jax.ShapeDtypeStruct((BATCH, SEQ, D_MODEL), F32),
        compiler_params=pltpu.CompilerParams(
            dimension_semantics=("parallel", "parallel"),
            vmem_limit_bytes=58 * 1024 * 1024),
        name="post",
    )(x, mix, mod, w_out, norm_g, w_gate, w_up, w_down)


def kernel(x, c, ada_w, ada_b, norm_mix_g, norm_ffn_g, ev_w_in, ev_conv_w, ev_ret_norm_g, ev_w_out,
           od_w_qkv, od_q_norm_g, od_k_norm_g, od_w_out, ffn_w_gate, ffn_w_up, ffn_w_down):
    bf = lambda w: w.astype(BF16)
    row3 = lambda g: g.reshape(g.shape[0], 1, g.shape[1])
    ev_w_in, ev_w_out, od_w_qkv, od_w_out = bf(ev_w_in), bf(ev_w_out), bf(od_w_qkv), bf(od_w_out)
    ffn_w_gate, ffn_w_up, ffn_w_down = bf(ffn_w_gate), bf(ffn_w_up), bf(ffn_w_down)
    norm_mix_g, norm_ffn_g = row3(norm_mix_g), row3(norm_ffn_g)
    ev_ret_norm_g, od_q_norm_g, od_k_norm_g = row3(ev_ret_norm_g), row3(od_q_norm_g), row3(od_k_norm_g)
    ev_conv_w = jnp.pad(ev_conv_w, ((0, 0), (0, V7X_SUBLANES - CONV_WIDTH), (0, 0)))
    mod = _adaln(c, ada_w, ada_b)
    tables = _retention_tables()
    for l in range(DEPTH):
        j = l // 2
        if l % 2 == 0:
            mix = _even_mix(x, mod, l, j, norm_mix_g, ev_w_in, ev_conv_w, ev_ret_norm_g, tables)
            w_out = ev_w_out
        else:
            q, k, v = _qkv(x, mod, l, j, norm_mix_g, od_w_qkv, od_q_norm_g, od_k_norm_g)
            mix = _sb_attention(q, k, v)
            w_out = od_w_out
        x = _post(x, mix, mod, l, j, w_out, norm_ffn_g, ffn_w_gate, ffn_w_up, ffn_w_down)
    return x
```

```python
import functools
import math

import jax
import jax.numpy as jnp
import numpy as np
from jax import lax
from jax.experimental import pallas as pl
from jax.experimental.pallas import tpu as pltpu

D_MODEL = 1024
BATCH = 4
SEQ = 4096
DEPTH = 4
CHUNK = 64
EPS = 1e-6
CONV_WIDTH = 3
CONV_DIM = 512
RET_HEADS = 4
RET_HEAD_DIM = 128
RET_DIM = 512
ROPE_THETA = 10000.0
SB_HEADS = 8
SB_HEAD_DIM = 128
SB_DIM = 1024
D_FF = 2816
EVEN_IN = 3 * CONV_DIM + 4 * RET_DIM

V7X_SUBLANES = 8
V7X_LANES = 128
V7X_MXU_DIM = 256
V7X_VMEM_BYTES = 64 * 1024 * 1024

LOG2E = 1.0 / math.log(2.0)

EVEN_TM = 512
RET_TILE = V7X_MXU_DIM
QKV_TM = 512
POST_TM = 512
SB_TK = V7X_MXU_DIM
MASKED_LOGIT = -1e30
SOFTPLUS_LINEAR = 64.0
F32_MIN_SUBNORMAL_LOG2 = -149.0
SB_EXIT_LOG2 = 160.0
assert SB_EXIT_LOG2 > -F32_MIN_SUBNORMAL_LOG2 + 1

F32 = jnp.float32
BF16 = jnp.bfloat16


def _sigmoid(x):
    return 1.0 / (1.0 + jnp.exp(-x))


def _norm_mod(x, gain, scale, shift):
    ms = jnp.mean(x * x, axis=-1, keepdims=True)
    return (x * lax.rsqrt(ms + EPS)) * (gain * (1.0 + scale)) + shift


def _mod_row(mod_ref, k, b):
    return mod_ref[0, k, pl.ds(b, 1), :]


def _adaln_kernel(c_ref, w_ref, b_ref, o_ref):
    c = c_ref[...]
    ca = c * _sigmoid(c)
    o_ref[0, 0] = jnp.dot(ca, w_ref[0], precision=lax.Precision.HIGHEST,
                          preferred_element_type=F32) + b_ref[0, 0]


def _adaln(c, ada_w, ada_b):
    c_pad = jnp.pad(c, ((0, V7X_SUBLANES - BATCH), (0, 0)))
    b4 = ada_b.reshape(DEPTH, 6, 1, D_MODEL)
    return pl.pallas_call(
        _adaln_kernel,
        grid=(DEPTH, 6),
        in_specs=[
            pl.BlockSpec((V7X_SUBLANES, D_MODEL), lambda l, k: (0, 0)),
            pl.BlockSpec((1, D_MODEL, D_MODEL), lambda l, k: (l, 0, k)),
            pl.BlockSpec((1, 1, 1, D_MODEL), lambda l, k: (l, k, 0, 0)),
        ],
        out_specs=pl.BlockSpec((1, 1, V7X_SUBLANES, D_MODEL), lambda l, k: (l, k, 0, 0)),
        out_shape=jax.ShapeDtypeStruct((DEPTH, 6, V7X_SUBLANES, D_MODEL), F32),
        compiler_params=pltpu.CompilerParams(dimension_semantics=("parallel", "parallel")),
        name="adaln",
    )(c_pad, ada_w, b4)


def _retention_tables():
    h = np.arange(RET_HEADS, dtype=np.float64)
    log_g = np.log1p(-np.exp2(-5.0 - h))
    idx = np.arange(RET_TILE, dtype=np.float64)
    diff = idx[:, None] - idx[None, :]
    same_or_earlier = (idx[None, :] // CHUNK) <= (idx[:, None] // CHUNK)
    k_scale = RET_HEAD_DIM ** -0.5
    dec = np.exp(np.abs(diff)[None] * log_g[:, None, None]) * same_or_earlier[None] * k_scale
    qdec = np.exp((idx + 1.0)[None, :] * log_g[:, None])
    kdec = np.exp((RET_TILE - 1.0 - idx)[None, :] * log_g[:, None]) * k_scale
    sdec = np.exp(RET_TILE * log_g)
    qdec = np.broadcast_to(qdec[:, :, None], (RET_HEADS, RET_TILE, V7X_LANES))
    kdec = np.broadcast_to(kdec[:, :, None], (RET_HEADS, RET_TILE, V7X_LANES))
    sdec = np.broadcast_to(sdec[:, None, None], (RET_HEADS, V7X_SUBLANES, V7X_LANES))
    inv_freq = 1.0 / (ROPE_THETA ** (np.arange(0, RET_HEAD_DIM, 2, dtype=np.float64) / RET_HEAD_DIM))
    ang = np.arange(SEQ, dtype=np.float64)[:, None] * inv_freq[None, :]
    cos2 = np.concatenate([np.cos(ang), np.cos(ang)], axis=-1)
    sin2 = np.concatenate([-np.sin(ang), np.sin(ang)], axis=-1)
    rot = np.stack([cos2, sin2])
    f = lambda a: jnp.asarray(np.ascontiguousarray(a), dtype=F32)
    return f(dec), f(qdec), f(kdec), f(sdec), f(rot)


def _even_kernel(x_ref, mod_ref, ng_ref, win_ref, cw_ref, rg_ref, rot_ref, dec_ref, qdec_ref,
                 kdec_ref, sdec_ref, mix_ref, state_ref, zbuf_ref):
    b = pl.program_id(0)
    si = pl.program_id(1)
    tm = EVEN_TM

    @pl.when(si == 0)
    def _():
        state_ref[...] = jnp.zeros_like(state_ref)
        zbuf_ref[0:V7X_SUBLANES, :] = jnp.zeros((V7X_SUBLANES, CONV_DIM), F32)

    x = x_ref[0]
    h = _norm_mod(x, ng_ref[0], _mod_row(mod_ref, 1, b), _mod_row(mod_ref, 0, b)).astype(BF16)
    proj = jnp.dot(h, win_ref[0], preferred_element_type=F32)

    b_gate = proj[:, 0:CONV_DIM]
    z = proj[:, CONV_DIM:2 * CONV_DIM] * proj[:, 2 * CONV_DIM:3 * CONV_DIM]
    zbuf_ref[V7X_SUBLANES:V7X_SUBLANES + tm, :] = z
    z1 = zbuf_ref[V7X_SUBLANES - 1:V7X_SUBLANES - 1 + tm, :]
    z2 = zbuf_ref[V7X_SUBLANES - 2:V7X_SUBLANES - 2 + tm, :]
    cw = cw_ref[0]
    y = cw[0:1, :] * z2 + cw[1:2, :] * z1 + cw[2:3, :] * z
    mix_ref[0, :, 0:CONV_DIM] = (b_gate * y).astype(BF16)
    zbuf_ref[0:V7X_SUBLANES, :] = zbuf_ref[tm:tm + V7X_SUBLANES, :]

    q_off = 3 * CONV_DIM
    k_off = q_off + RET_DIM
    v_off = k_off + RET_DIM
    g_off = v_off + RET_DIM
    for r in range(tm // RET_TILE):
        r0 = r * RET_TILE
        cos = rot_ref[0, r0:r0 + RET_TILE, :]
        sin = rot_ref[1, r0:r0 + RET_TILE, :]
        for hh in range(RET_HEADS):
            c0 = hh * RET_HEAD_DIM
            q = proj[r0:r0 + RET_TILE, q_off + c0:q_off + c0 + RET_HEAD_DIM]
            k = proj[r0:r0 + RET_TILE, k_off + c0:k_off + c0 + RET_HEAD_DIM]
            v = proj[r0:r0 + RET_TILE, v_off + c0:v_off + c0 + RET_HEAD_DIM].astype(BF16)
            g = proj[r0:r0 + RET_TILE, g_off + c0:g_off + c0 + RET_HEAD_DIM]
            qr = q * cos + pltpu.roll(q, RET_HEAD_DIM // 2, axis=1) * sin
            kr = k * cos + pltpu.roll(k, RET_HEAD_DIM // 2, axis=1) * sin
            s = lax.dot_general(qr.astype(BF16), kr.astype(BF16), (((1,), (1,)), ((), ())),
                                preferred_element_type=F32)
            s = s * dec_ref[hh]
            o = jnp.dot(s.astype(BF16), v, preferred_element_type=F32)
            st = state_ref[hh]
            o = o + jnp.dot((qr * qdec_ref[hh]).astype(BF16), st.astype(BF16),
                            preferred_element_type=F32)
            kd = (kr * kdec_ref[hh]).astype(BF16)
            kv = lax.dot_general(kd, v, (((0,), (0,)), ((), ())), preferred_element_type=F32)
            state_ref[hh] = st * sdec_ref[hh, 0:1, :] + kv
            ms = jnp.mean(o * o, axis=-1, keepdims=True)
            rn = (o * lax.rsqrt(ms + EPS)) * rg_ref[0, :, c0:c0 + RET_HEAD_DIM]
            out = (g * _sigmoid(g)) * rn
            mix_ref[0, r0:r0 + RET_TILE, CONV_DIM + c0:CONV_DIM + c0 + RET_HEAD_DIM] = out.astype(BF16)


def _even_mix(x, mod, l, j, norm_g, w_in, conv_w, ret_g, tables):
    dec, qdec, kdec, sdec, rot = tables
    tm = EVEN_TM
    const3 = lambda b, s: (0, 0, 0)
    layer = lambda b, s: (l, 0, 0)
    even = lambda b, s: (j, 0, 0)
    return pl.pallas_call(
        _even_kernel,
        grid=(BATCH, SEQ // tm),
        in_specs=[
            pl.BlockSpec((1, tm, D_MODEL), lambda b, s: (b, s, 0)),
            pl.BlockSpec((1, 6, V7X_SUBLANES, D_MODEL), lambda b, s: (l, 0, 0, 0)),
            pl.BlockSpec((1, 1, D_MODEL), layer),
            pl.BlockSpec((1, D_MODEL, EVEN_IN), even, pipeline_mode=pl.Buffered(1)),
            pl.BlockSpec((1, V7X_SUBLANES, CONV_DIM), even),
            pl.BlockSpec((1, 1, RET_DIM), even),
            pl.BlockSpec((2, tm, RET_HEAD_DIM), lambda b, s: (0, s, 0)),
            pl.BlockSpec((RET_HEADS, RET_TILE, RET_TILE), const3),
            pl.BlockSpec((RET_HEADS, RET_TILE, V7X_LANES), const3),
            pl.BlockSpec((RET_HEADS, RET_TILE, V7X_LANES), const3),
            pl.BlockSpec((RET_HEADS, V7X_SUBLANES, V7X_LANES), const3),
        ],
        out_specs=pl.BlockSpec((1, tm, D_MODEL), lambda b, s: (b, s, 0)),
        out_shape=jax.ShapeDtypeStruct((BATCH, SEQ, D_MODEL), BF16),
        scratch_shapes=[
            pltpu.VMEM((RET_HEADS, RET_HEAD_DIM, RET_HEAD_DIM), F32),
            pltpu.VMEM((tm + V7X_SUBLANES, CONV_DIM), F32),
        ],
        compiler_params=pltpu.CompilerParams(
            dimension_semantics=("arbitrary", "arbitrary"),
            vmem_limit_bytes=56 * 1024 * 1024),
        name="even_mix",
    )(x, mod, norm_g, w_in, conv_w, ret_g, rot, dec, qdec, kdec, sdec)


def _qkv_kernel(x_ref, mod_ref, ng_ref, w_ref, qg_ref, kg_ref, q_ref, k_ref, v_ref):
    b = pl.program_id(0)
    x = x_ref[0]
    h = _norm_mod(x, ng_ref[0], _mod_row(mod_ref, 1, b), _mod_row(mod_ref, 0, b)).astype(BF16)
    qkv = jnp.dot(h, w_ref[0], preferred_element_type=F32)
    qg = qg_ref[0] * (SB_HEAD_DIM ** -0.5 * LOG2E)
    kg = kg_ref[0]
    for hh in range(SB_HEADS):
        c0 = hh * SB_HEAD_DIM
        q = qkv[:, c0:c0 + SB_HEAD_DIM]
        k = qkv[:, SB_DIM + c0:SB_DIM + c0 + SB_HEAD_DIM]
        v = qkv[:, 2 * SB_DIM + c0:2 * SB_DIM + c0 + SB_HEAD_DIM]
        qn = (q * lax.rsqrt(jnp.mean(q * q, axis=-1, keepdims=True) + EPS)) * qg
        kn = (k * lax.rsqrt(jnp.mean(k * k, axis=-1, keepdims=True) + EPS)) * kg
        q_ref[0, hh] = qn.astype(BF16)
        k_ref[0, hh] = kn.astype(BF16)
        v_ref[0, hh] = v.astype(BF16)


def _qkv(x, mod, l, j, norm_g, w_qkv, q_g, k_g):
    tm = QKV_TM
    layer = lambda b, s: (l, 0, 0)
    odd = lambda b, s: (j, 0, 0)
    head_spec = pl.BlockSpec((1, SB_HEADS, tm, SB_HEAD_DIM), lambda b, s: (b, 0, s, 0))
    head_shape = jax.ShapeDtypeStruct((BATCH, SB_HEADS, SEQ, SB_HEAD_DIM), BF16)
    return pl.pallas_call(
        _qkv_kernel,
        grid=(BATCH, SEQ // tm),
        in_specs=[
            pl.BlockSpec((1, tm, D_MODEL), lambda b, s: (b, s, 0)),
            pl.BlockSpec((1, 6, V7X_SUBLANES, D_MODEL), lambda b, s: (l, 0, 0, 0)),
            pl.BlockSpec((1, 1, D_MODEL), layer),
            pl.BlockSpec((1, D_MODEL, 3 * SB_DIM), odd, pipeline_mode=pl.Buffered(1)),
            pl.BlockSpec((1, 1, SB_HEAD_DIM), odd),
            pl.BlockSpec((1, 1, SB_HEAD_DIM), odd),
        ],
        out_specs=[head_spec, head_spec, head_spec],
        out_shape=[head_shape, head_shape, head_shape],
        compiler_params=pltpu.CompilerParams(
            dimension_semantics=("parallel", "parallel"),
            vmem_limit_bytes=56 * 1024 * 1024),
        name="qkv",
    )(x, mod, norm_g, w_qkv, q_g, k_g)


def _softplus2(z):
    return jnp.maximum(z, jnp.log(1.0 + jnp.exp2(jnp.minimum(z, SOFTPLUS_LINEAR))) * LOG2E)


def _sb_kernel(q_ref, k_ref, v_ref, o_ref, spb0, spb1, lb0, lb1, a0, a1, rc_ref, ac_ref,
               acc_all, r_all):
    row = lax.broadcasted_iota(jnp.int32, (SB_TK, SB_TK), 0)
    col = lax.broadcasted_iota(jnp.int32, (SB_TK, SB_TK), 1)
    later = (row > col).astype(BF16)
    causal = col < row
    sets = ((spb0, lb0), (spb1, lb1))
    abuf = (a0, a1)
    nblk = SEQ // SB_TK

    def blk(ref, j, rows=SB_TK):
        return ref[0, 0, pl.ds(pl.multiple_of(j * SB_TK, SB_TK), rows), :]

    def qk(q, j):
        return lax.dot_general(q, blk(k_ref, j), (((1,), (1,)), ((), ())),
                               preferred_element_type=F32)

    def stage_a(j, i, m):
        spb_ref, lb_ref = sets[i]
        z = qk(blk(q_ref, j, m), j)
        z_top = jnp.where(causal, z[:SB_TK], MASKED_LOGIT)
        sp_top = _softplus2(z_top)
        spb_ref[0:SB_TK] = sp_top.astype(BF16)
        lb_ref[0:SB_TK] = z_top - sp_top
        if m > SB_TK:
            sp = _softplus2(z[SB_TK:])
            spb_ref[SB_TK:m] = sp.astype(BF16)
            lb_ref[SB_TK:m] = z[SB_TK:] - sp

    def stage_b(j, i, m):
        spb_ref, lb_ref = sets[i]
        cs = jnp.dot(spb_ref[0:m], later, preferred_element_type=F32)
        rowsum = cs[:, 0:1] + spb_ref[0:m, 0:1].astype(F32)
        abuf[i][0:SB_TK] = lb_ref[0:SB_TK] - cs[:SB_TK]
        if m > SB_TK:
            r_old = rc_ref[...]
            abuf[i][SB_TK:m] = (lb_ref[SB_TK:m] - cs[SB_TK:]) - r_old
            r_all[pl.ds(pl.multiple_of((j + 1) * SB_TK, SB_TK), SB_TK), :] = r_old + rowsum[SB_TK:]
        rc_ref[...] = rowsum[:SB_TK]

    def stage_c(j, i, m):
        w = jnp.exp2(abuf[i][0:m]).astype(BF16)
        pv = jnp.dot(w, blk(v_ref, j), preferred_element_type=F32)
        if m > SB_TK:
            acc_all[pl.ds(pl.multiple_of((j + 1) * SB_TK, SB_TK), SB_TK), :] = ac_ref[...] + pv[SB_TK:]
        ac_ref[...] = pv[:SB_TK]

    def tick(tau, par, a=True, b=True, c=True):
        rows = lambda t: SB_TK if (isinstance(t, int) and t == 0) else 2 * SB_TK
        if a:
            stage_a(nblk - 1 - tau, par, rows(tau))
        if b:
            stage_b(nblk - tau, 1 - par, rows(tau - 1))
        if c:
            stage_c(nblk + 1 - tau, par, rows(tau - 2))

    for tau in range(nblk + 2):
        tick(tau, tau % 2, a=tau < nblk, b=1 <= tau <= nblk, c=tau >= 2)
    acc_all[0:SB_TK] = ac_ref[...]

    r_all[0:2 * SB_TK] = jnp.full((2 * SB_TK, 1), 2 * SB_EXIT_LOG2, F32)

    @pl.when(jnp.min(r_all[...]) <= SB_EXIT_LOG2)
    def _():
        def per_block(m, carry):
            rows = pl.ds(pl.multiple_of(m * SB_TK, SB_TK), SB_TK)

            def step(c2):
                j, _ = c2
                z = qk(blk(q_ref, m), j)
                sp = _softplus2(z)
                spb = sp.astype(BF16)
                cs = jnp.dot(spb, later, preferred_element_type=F32)
                r = r_all[rows, :]
                w = jnp.exp2(((z - sp) - cs) - r).astype(BF16)
                acc_all[rows, :] += jnp.dot(w, blk(v_ref, j), preferred_element_type=F32)
                r = r + (cs[:, 0:1] + spb[:, 0:1].astype(F32))
                r_all[rows, :] = r
                return j - 1, jnp.min(r)

            def more(c2):
                j, r_min = c2
                return jnp.logical_and(j >= 0, r_min <= SB_EXIT_LOG2)

            lax.while_loop(more, step, (m - 2, jnp.min(r_all[rows, :])))
            return carry

        lax.fori_loop(2, nblk, per_block, 0)

    o_ref[0] = acc_all[...].astype(BF16)


def _sb_attention(q, k, v):
    head_spec = pl.BlockSpec((1, 1, SEQ, SB_HEAD_DIM), lambda b, h: (b, h, 0, 0))
    return pl.pallas_call(
        _sb_kernel,
        grid=(BATCH, SB_HEADS),
        in_specs=[head_spec, head_spec, head_spec],
        out_specs=pl.BlockSpec((1, SEQ, SB_HEAD_DIM), lambda b, h: (b, 0, h)),
        out_shape=jax.ShapeDtypeStruct((BATCH, SEQ, SB_DIM), BF16),
        scratch_shapes=[
            pltpu.VMEM((2 * SB_TK, SB_TK), BF16), pltpu.VMEM((2 * SB_TK, SB_TK), BF16),
            pltpu.VMEM((2 * SB_TK, SB_TK), F32), pltpu.VMEM((2 * SB_TK, SB_TK), F32),
            pltpu.VMEM((2 * SB_TK, SB_TK), F32), pltpu.VMEM((2 * SB_TK, SB_TK), F32),
            pltpu.VMEM((SB_TK, 1), F32), pltpu.VMEM((SB_TK, SB_HEAD_DIM), F32),
            pltpu.VMEM((SEQ, SB_HEAD_DIM), F32), pltpu.VMEM((SEQ, 1), F32),
        ],
        compiler_params=pltpu.CompilerParams(dimension_semantics=("parallel", "parallel")),
        name="sb_attn",
    )(q, k, v)


def _post_kernel(x_ref, mix_ref, mod_ref, wo_ref, ng_ref, wg_ref, wu_ref, wd_ref, o_ref):
    b = pl.program_id(0)
    x = x_ref[0]
    y = jnp.dot(mix_ref[0], wo_ref[0], preferred_element_type=F32)
    x1 = x + _mod_row(mod_ref, 2, b) * y
    h = _norm_mod(x1, ng_ref[0], _mod_row(mod_ref, 4, b), _mod_row(mod_ref, 3, b)).astype(BF16)
    gate = jnp.dot(h, wg_ref[0], preferred_element_type=F32)
    up = jnp.dot(h, wu_ref[0], preferred_element_type=F32)
    act = ((gate * _sigmoid(gate)) * up).astype(BF16)
    ffn = jnp.dot(act, wd_ref[0], preferred_element_type=F32)
    o_ref[0] = x1 + _mod_row(mod_ref, 5, b) * ffn


def _post(x, mix, mod, l, j, w_out, norm_g, w_gate, w_up, w_down):
    tm = POST_TM
    layer = lambda b, s: (l, 0, 0)
    single = pl.Buffered(1)
    return pl.pallas_call(
        _post_kernel,
        grid=(BATCH, SEQ // tm),
        in_specs=[
            pl.BlockSpec((1, tm, D_MODEL), lambda b, s: (b, s, 0)),
            pl.BlockSpec((1, tm, D_MODEL), lambda b, s: (b, s, 0)),
            pl.BlockSpec((1, 6, V7X_SUBLANES, D_MODEL), lambda b, s: (l, 0, 0, 0)),
            pl.BlockSpec((1, D_MODEL, D_MODEL), lambda b, s: (j, 0, 0), pipeline_mode=single),
            pl.BlockSpec((1, 1, D_MODEL), layer),
            pl.BlockSpec((1, D_MODEL, D_FF), layer, pipeline_mode=single),
            pl.BlockSpec((1, D_MODEL, D_FF), layer, pipeline_mode=single),
            pl.BlockSpec((1, D_FF, D_MODEL), layer, pipeline_mode=single),
        ],
        out_specs=pl.BlockSpec((1, tm, D_MODEL), lambda b, s: (b, s, 0)),
        out_shape=jax.ShapeDtypeStruct((BATCH, SEQ, D_MODEL), F32),
        compiler_params=pltpu.CompilerParams(
            dimension_semantics=("parallel", "parallel"),
            vmem_limit_bytes=58 * 1024 * 1024),
        name="post",
    )(x, mix, mod, w_out, norm_g, w_gate, w_up, w_down)


def kernel(x, c, ada_w, ada_b, norm_mix_g, norm_ffn_g, ev_w_in, ev_conv_w, ev_ret_norm_g, ev_w_out,
           od_w_qkv, od_q_norm_g, od_k_norm_g, od_w_out, ffn_w_gate, ffn_w_up, ffn_w_down):
    bf = lambda w: w.astype(BF16)
    row3 = lambda g: g.reshape(g.shape[0], 1, g.shape[1])
    ev_w_in, ev_w_out, od_w_qkv, od_w_out = bf(ev_w_in), bf(ev_w_out), bf(od_w_qkv), bf(od_w_out)
    ffn_w_gate, ffn_w_up, ffn_w_down = bf(ffn_w_gate), bf(ffn_w_up), bf(ffn_w_down)
    norm_mix_g, norm_ffn_g = row3(norm_mix_g), row3(norm_ffn_g)
    ev_ret_norm_g, od_q_norm_g, od_k_norm_g = row3(ev_ret_norm_g), row3(od_q_norm_g), row3(od_k_norm_g)
    ev_conv_w = jnp.pad(ev_conv_w, ((0, 0), (0, V7X_SUBLANES - CONV_WIDTH), (0, 0)))
    mod = _adaln(c, ada_w, ada_b)
    tables = _retention_tables()
    for l in range(DEPTH):
        j = l // 2
        if l % 2 == 0:
            mix = _even_mix(x, mod, l, j, norm_mix_g, ev_w_in, ev_conv_w, ev_ret_norm_g, tables)
            w_out = ev_w_out
        else:
            q, k, v = _qkv(x, mod, l, j, norm_mix_g, od_w_qkv, od_q_norm_g, od_k_norm_g)
            mix = _sb_attention(q, k, v)
            w_out = od_w_out
        x = _post(x, mix, mod, l, j, w_out, norm_ffn_g, ffn_w_gate, ffn_w_up, ffn_w_down)
    return x
```

```python
import functools
import math

import jax
import jax.numpy as jnp
import numpy as np
from jax import lax
from jax.experimental import pallas as pl
from jax.experimental.pallas import tpu as pltpu

D_MODEL = 1024
BATCH = 4
SEQ = 4096
DEPTH = 4
CHUNK = 64
EPS = 1e-6
CONV_WIDTH = 3
CONV_DIM = 512
RET_HEADS = 4
RET_HEAD_DIM = 128
RET_DIM = 512
ROPE_THETA = 10000.0
SB_HEADS = 8
SB_HEAD_DIM = 128
SB_DIM = 1024
D_FF = 2816
EVEN_IN = 3 * CONV_DIM + 4 * RET_DIM

V7X_SUBLANES = 8
V7X_BF16_SUBLANES = 16
V7X_LANES = 128
V7X_MXU_DIM = 256
V7X_VMEM_BYTES = 64 * 1024 * 1024

LOG2E = 1.0 / math.log(2.0)

EVEN_TM = 512
RET_TILE = V7X_MXU_DIM
QKV_TM = 512
POST_TM = 512
SB_TK = V7X_MXU_DIM
MASKED_LOGIT = -1e30
SOFTPLUS_LINEAR = 64.0
F32_MIN_SUBNORMAL_LOG2 = -149.0
SB_EXIT_LOG2 = 160.0
assert SB_EXIT_LOG2 > -F32_MIN_SUBNORMAL_LOG2 + 1

F32 = jnp.float32
BF16 = jnp.bfloat16


def _sigmoid(x):
    return 1.0 / (1.0 + jnp.exp(-x))


def _norm_mod(x, gain, scale, shift):
    ms = jnp.mean(x * x, axis=-1, keepdims=True)
    return (x * lax.rsqrt(ms + EPS)) * (gain * (1.0 + scale)) + shift


def _mod_row(mod_ref, k, b):
    return mod_ref[0, k, pl.ds(b, 1), :]


def _adaln_kernel(c_ref, w_ref, b_ref, o_ref):
    c = c_ref[...]
    ca = c * _sigmoid(c)
    o_ref[0, 0] = jnp.dot(ca, w_ref[0], precision=lax.Precision.HIGHEST,
                          preferred_element_type=F32) + b_ref[0, 0]


def _adaln(c, ada_w, ada_b):
    c_pad = jnp.pad(c, ((0, V7X_SUBLANES - BATCH), (0, 0)))
    b4 = ada_b.reshape(DEPTH, 6, 1, D_MODEL)
    return pl.pallas_call(
        _adaln_kernel,
        grid=(DEPTH, 6),
        in_specs=[
            pl.BlockSpec((V7X_SUBLANES, D_MODEL), lambda l, k: (0, 0)),
            pl.BlockSpec((1, D_MODEL, D_MODEL), lambda l, k: (l, 0, k)),
            pl.BlockSpec((1, 1, 1, D_MODEL), lambda l, k: (l, k, 0, 0)),
        ],
        out_specs=pl.BlockSpec((1, 1, V7X_SUBLANES, D_MODEL), lambda l, k: (l, k, 0, 0)),
        out_shape=jax.ShapeDtypeStruct((DEPTH, 6, V7X_SUBLANES, D_MODEL), F32),
        compiler_params=pltpu.CompilerParams(dimension_semantics=("parallel", "parallel")),
        name="adaln",
    )(c_pad, ada_w, b4)


def _retention_tables():
    h = np.arange(RET_HEADS, dtype=np.float64)
    log_g = np.log1p(-np.exp2(-5.0 - h))
    idx = np.arange(RET_TILE, dtype=np.float64)
    diff = idx[:, None] - idx[None, :]
    same_or_earlier = (idx[None, :] // CHUNK) <= (idx[:, None] // CHUNK)
    k_scale = RET_HEAD_DIM ** -0.5
    dec = np.exp(np.abs(diff)[None] * log_g[:, None, None]) * same_or_earlier[None] * k_scale
    qdec = np.exp((idx + 1.0)[None, :] * log_g[:, None])
    kdec = np.exp((RET_TILE - 1.0 - idx)[None, :] * log_g[:, None]) * k_scale
    sdec = np.exp(RET_TILE * log_g)
    qdec = np.broadcast_to(qdec[:, :, None], (RET_HEADS, RET_TILE, V7X_LANES))
    kdec = np.broadcast_to(kdec[:, :, None], (RET_HEADS, RET_TILE, V7X_LANES))
    sdec = np.broadcast_to(sdec[:, None, None], (RET_HEADS, V7X_SUBLANES, V7X_LANES))
    inv_freq = 1.0 / (ROPE_THETA ** (np.arange(0, RET_HEAD_DIM, 2, dtype=np.float64) / RET_HEAD_DIM))
    ang = np.arange(SEQ, dtype=np.float64)[:, None] * inv_freq[None, :]
    cos2 = np.concatenate([np.cos(ang), np.cos(ang)], axis=-1)
    sin2 = np.concatenate([-np.sin(ang), np.sin(ang)], axis=-1)
    rot = np.stack([cos2, sin2])
    f = lambda a: jnp.asarray(np.ascontiguousarray(a), dtype=F32)
    return f(dec), f(qdec), f(kdec), f(sdec), f(rot)


def _even_kernel(x_ref, mod_ref, ng_ref, win_ref, cw_ref, rg_ref, rot_ref, dec_ref, qdec_ref,
                 kdec_ref, sdec_ref, mix_ref, state_ref, zbuf_ref):
    b = pl.program_id(0)
    si = pl.program_id(1)
    tm = EVEN_TM

    @pl.when(si == 0)
    def _():
        state_ref[...] = jnp.zeros_like(state_ref)
        zbuf_ref[0:V7X_SUBLANES, :] = jnp.zeros((V7X_SUBLANES, CONV_DIM), F32)

    x = x_ref[0]
    h = _norm_mod(x, ng_ref[0], _mod_row(mod_ref, 1, b), _mod_row(mod_ref, 0, b)).astype(BF16)
    proj = jnp.dot(h, win_ref[0], preferred_element_type=F32)

    b_gate = proj[:, 0:CONV_DIM]
    z = proj[:, CONV_DIM:2 * CONV_DIM] * proj[:, 2 * CONV_DIM:3 * CONV_DIM]
    zbuf_ref[V7X_SUBLANES:V7X_SUBLANES + tm, :] = z
    z1 = zbuf_ref[V7X_SUBLANES - 1:V7X_SUBLANES - 1 + tm, :]
    z2 = zbuf_ref[V7X_SUBLANES - 2:V7X_SUBLANES - 2 + tm, :]
    cw = cw_ref[0]
    y = cw[0:1, :] * z2 + cw[1:2, :] * z1 + cw[2:3, :] * z
    mix_ref[0, :, 0:CONV_DIM] = (b_gate * y).astype(BF16)
    zbuf_ref[0:V7X_SUBLANES, :] = zbuf_ref[tm:tm + V7X_SUBLANES, :]

    q_off = 3 * CONV_DIM
    k_off = q_off + RET_DIM
    v_off = k_off + RET_DIM
    g_off = v_off + RET_DIM
    for r in range(tm // RET_TILE):
        r0 = r * RET_TILE
        cos = rot_ref[0, r0:r0 + RET_TILE, :]
        sin = rot_ref[1, r0:r0 + RET_TILE, :]
        for hh in range(RET_HEADS):
            c0 = hh * RET_HEAD_DIM
            q = proj[r0:r0 + RET_TILE, q_off + c0:q_off + c0 + RET_HEAD_DIM]
            k = proj[r0:r0 + RET_TILE, k_off + c0:k_off + c0 + RET_HEAD_DIM]
            v = proj[r0:r0 + RET_TILE, v_off + c0:v_off + c0 + RET_HEAD_DIM].astype(BF16)
            g = proj[r0:r0 + RET_TILE, g_off + c0:g_off + c0 + RET_HEAD_DIM]
            qr = q * cos + pltpu.roll(q, RET_HEAD_DIM // 2, axis=1) * sin
            kr = k * cos + pltpu.roll(k, RET_HEAD_DIM // 2, axis=1) * sin
            s = lax.dot_general(qr.astype(BF16), kr.astype(BF16), (((1,), (1,)), ((), ())),
                                preferred_element_type=F32)
            s = s * dec_ref[hh]
            o = jnp.dot(s.astype(BF16), v, preferred_element_type=F32)
            st = state_ref[hh]
            o = o + jnp.dot((qr * qdec_ref[hh]).astype(BF16), st.astype(BF16),
                            preferred_element_type=F32)
            kd = (kr * kdec_ref[hh]).astype(BF16)
            kv = lax.dot_general(kd, v, (((0,), (0,)), ((), ())), preferred_element_type=F32)
            state_ref[hh] = st * sdec_ref[hh, 0:1, :] + kv
            ms = jnp.mean(o * o, axis=-1, keepdims=True)
            rn = (o * lax.rsqrt(ms + EPS)) * rg_ref[0, :, c0:c0 + RET_HEAD_DIM]
            out = (g * _sigmoid(g)) * rn
            mix_ref[0, r0:r0 + RET_TILE, CONV_DIM + c0:CONV_DIM + c0 + RET_HEAD_DIM] = out.astype(BF16)


def _even_mix(x, mod, l, j, norm_g, w_in, conv_w, ret_g, tables):
    dec, qdec, kdec, sdec, rot = tables
    tm = EVEN_TM
    const3 = lambda b, s: (0, 0, 0)
    layer = lambda b, s: (l, 0, 0)
    even = lambda b, s: (j, 0, 0)
    return pl.pallas_call(
        _even_kernel,
        grid=(BATCH, SEQ // tm),
        in_specs=[
            pl.BlockSpec((1, tm, D_MODEL), lambda b, s: (b, s, 0)),
            pl.BlockSpec((1, 6, V7X_SUBLANES, D_MODEL), lambda b, s: (l, 0, 0, 0)),
            pl.BlockSpec((1, 1, D_MODEL), layer),
            pl.BlockSpec((1, D_MODEL, EVEN_IN), const3, pipeline_mode=pl.Buffered(1)),
            pl.BlockSpec((1, V7X_SUBLANES, CONV_DIM), even),
            pl.BlockSpec((1, 1, RET_DIM), even),
            pl.BlockSpec((2, tm, RET_HEAD_DIM), lambda b, s: (0, s, 0)),
            pl.BlockSpec((RET_HEADS, RET_TILE, RET_TILE), const3),
            pl.BlockSpec((RET_HEADS, RET_TILE, V7X_LANES), const3),
            pl.BlockSpec((RET_HEADS, RET_TILE, V7X_LANES), const3),
            pl.BlockSpec((RET_HEADS, V7X_SUBLANES, V7X_LANES), const3),
        ],
        out_specs=pl.BlockSpec((1, tm, D_MODEL), lambda b, s: (b, s, 0)),
        out_shape=jax.ShapeDtypeStruct((BATCH, SEQ, D_MODEL), BF16),
        scratch_shapes=[
            pltpu.VMEM((RET_HEADS, RET_HEAD_DIM, RET_HEAD_DIM), F32),
            pltpu.VMEM((tm + V7X_SUBLANES, CONV_DIM), F32),
        ],
        compiler_params=pltpu.CompilerParams(
            dimension_semantics=("arbitrary", "arbitrary"),
            vmem_limit_bytes=56 * 1024 * 1024),
        name="even_mix",
    )(x, mod, norm_g, w_in, conv_w, ret_g, rot, dec, qdec, kdec, sdec)


def _qkv_kernel(x_ref, mod_ref, ng_ref, w_ref, qg_ref, kg_ref, q_ref, k_ref, v_ref):
    b = pl.program_id(0)
    x = x_ref[0]
    h = _norm_mod(x, ng_ref[0], _mod_row(mod_ref, 1, b), _mod_row(mod_ref, 0, b)).astype(BF16)
    qkv = jnp.dot(h, w_ref[0], preferred_element_type=F32)
    qg = qg_ref[0] * (SB_HEAD_DIM ** -0.5 * LOG2E)
    kg = kg_ref[0]
    for hh in range(SB_HEADS):
        c0 = hh * SB_HEAD_DIM
        q = qkv[:, c0:c0 + SB_HEAD_DIM]
        k = qkv[:, SB_DIM + c0:SB_DIM + c0 + SB_HEAD_DIM]
        v = qkv[:, 2 * SB_DIM + c0:2 * SB_DIM + c0 + SB_HEAD_DIM]
        qn = (q * lax.rsqrt(jnp.mean(q * q, axis=-1, keepdims=True) + EPS)) * qg
        kn = (k * lax.rsqrt(jnp.mean(k * k, axis=-1, keepdims=True) + EPS)) * kg
        q_ref[0, hh] = qn.astype(BF16)
        k_ref[0, hh] = kn.astype(BF16)
        v_ref[0, hh] = v.astype(BF16)


def _qkv(x, mod, l, j, norm_g, w_qkv, q_g, k_g):
    tm = QKV_TM
    layer = lambda b, s: (l, 0, 0)
    odd = lambda b, s: (j, 0, 0)
    head_spec = pl.BlockSpec((1, SB_HEADS, tm, SB_HEAD_DIM), lambda b, s: (b, 0, s, 0))
    head_shape = jax.ShapeDtypeStruct((BATCH, SB_HEADS, SEQ, SB_HEAD_DIM), BF16)
    return pl.pallas_call(
        _qkv_kernel,
        grid=(BATCH, SEQ // tm),
        in_specs=[
            pl.BlockSpec((1, tm, D_MODEL), lambda b, s: (b, s, 0)),
            pl.BlockSpec((1, 6, V7X_SUBLANES, D_MODEL), lambda b, s: (l, 0, 0, 0)),
            pl.BlockSpec((1, 1, D_MODEL), layer),
            pl.BlockSpec((1, D_MODEL, 3 * SB_DIM), lambda b, s: (0, 0, 0),
                         pipeline_mode=pl.Buffered(1)),
            pl.BlockSpec((1, 1, SB_HEAD_DIM), odd),
            pl.BlockSpec((1, 1, SB_HEAD_DIM), odd),
        ],
        out_specs=[head_spec, head_spec, head_spec],
        out_shape=[head_shape, head_shape, head_shape],
        compiler_params=pltpu.CompilerParams(
            dimension_semantics=("parallel", "parallel"),
            vmem_limit_bytes=56 * 1024 * 1024),
        name="qkv",
    )(x, mod, norm_g, w_qkv, q_g, k_g)


def _softplus2(z):
    return jnp.maximum(z, jnp.log(1.0 + jnp.exp2(jnp.minimum(z, SOFTPLUS_LINEAR))) * LOG2E)


def _sb_kernel(q_ref, k_ref, v_ref, o_ref, spb0, spb1, lb0, lb1, a0, a1, rc_ref, ac_ref,
               acc_all, r_all):
    row = lax.broadcasted_iota(jnp.int32, (SB_TK, SB_TK), 0)
    col = lax.broadcasted_iota(jnp.int32, (SB_TK, SB_TK), 1)
    later = (row > col).astype(BF16)
    causal = col < row
    sets = ((spb0, lb0), (spb1, lb1))
    abuf = (a0, a1)
    nblk = SEQ // SB_TK

    def blk(ref, j, rows=SB_TK):
        return ref[0, 0, pl.ds(pl.multiple_of(j * SB_TK, SB_TK), rows), :]

    def qk(q, j):
        return lax.dot_general(q, blk(k_ref, j), (((1,), (1,)), ((), ())),
                               preferred_element_type=F32)

    def stage_a(j, i, m):
        spb_ref, lb_ref = sets[i]
        z = qk(blk(q_ref, j, m), j)
        z_top = jnp.where(causal, z[:SB_TK], MASKED_LOGIT)
        sp_top = _softplus2(z_top)
        spb_ref[0:SB_TK] = sp_top.astype(BF16)
        lb_ref[0:SB_TK] = z_top - sp_top
        if m > SB_TK:
            sp = _softplus2(z[SB_TK:])
            spb_ref[SB_TK:m] = sp.astype(BF16)
            lb_ref[SB_TK:m] = z[SB_TK:] - sp

    def stage_b(j, i, m):
        spb_ref, lb_ref = sets[i]
        cs = jnp.dot(spb_ref[0:m], later, preferred_element_type=F32)
        rowsum = cs[:, 0:1] + spb_ref[0:m, 0:1].astype(F32)
        abuf[i][0:SB_TK] = lb_ref[0:SB_TK] - cs[:SB_TK]
        if m > SB_TK:
            r_old = rc_ref[...]
            abuf[i][SB_TK:m] = (lb_ref[SB_TK:m] - cs[SB_TK:]) - r_old
            r_all[pl.ds(pl.multiple_of((j + 1) * SB_TK, SB_TK), SB_TK), :] = r_old + rowsum[SB_TK:]
        rc_ref[...] = rowsum[:SB_TK]

    def stage_c(j, i, m):
        w = jnp.exp2(abuf[i][0:m]).astype(BF16)
        pv = jnp.dot(w, blk(v_ref, j), preferred_element_type=F32)
        if m > SB_TK:
            acc_all[pl.ds(pl.multiple_of((j + 1) * SB_TK, SB_TK), SB_TK), :] = ac_ref[...] + pv[SB_TK:]
        ac_ref[...] = pv[:SB_TK]

    def tick(tau, par, a=True, b=True, c=True):
        rows = lambda t: SB_TK if (isinstance(t, int) and t == 0) else 2 * SB_TK
        if a:
            stage_a(nblk - 1 - tau, par, rows(tau))
        if b:
            stage_b(nblk - tau, 1 - par, rows(tau - 1))
        if c:
            stage_c(nblk + 1 - tau, par, rows(tau - 2))

    for tau in range(nblk + 2):
        tick(tau, tau % 2, a=tau < nblk, b=1 <= tau <= nblk, c=tau >= 2)
    acc_all[0:SB_TK] = ac_ref[...]

    r_all[0:2 * SB_TK] = jnp.full((2 * SB_TK, 1), 2 * SB_EXIT_LOG2, F32)

    @pl.when(jnp.min(r_all[...]) <= SB_EXIT_LOG2)
    def _():
        def per_block(m, carry):
            rows = pl.ds(pl.multiple_of(m * SB_TK, SB_TK), SB_TK)

            def step(c2):
                j, _ = c2
                z = qk(blk(q_ref, m), j)
                sp = _softplus2(z)
                spb = sp.astype(BF16)
                cs = jnp.dot(spb, later, preferred_element_type=F32)
                r = r_all[rows, :]
                w = jnp.exp2(((z - sp) - cs) - r).astype(BF16)
                acc_all[rows, :] += jnp.dot(w, blk(v_ref, j), preferred_element_type=F32)
                r = r + (cs[:, 0:1] + spb[:, 0:1].astype(F32))
                r_all[rows, :] = r
                return j - 1, jnp.min(r)

            def more(c2):
                j, r_min = c2
                return jnp.logical_and(j >= 0, r_min <= SB_EXIT_LOG2)

            lax.while_loop(more, step, (m - 2, jnp.min(r_all[rows, :])))
            return carry

        lax.fori_loop(2, nblk, per_block, 0)

    o_ref[0] = acc_all[...].astype(BF16)


def _sb_attention(q, k, v):
    head_spec = pl.BlockSpec((1, 1, SEQ, SB_HEAD_DIM), lambda b, h: (b, h, 0, 0))
    return pl.pallas_call(
        _sb_kernel,
        grid=(BATCH, SB_HEADS),
        in_specs=[head_spec, head_spec, head_spec],
        out_specs=pl.BlockSpec((1, SEQ, SB_HEAD_DIM), lambda b, h: (b, 0, h)),
        out_shape=jax.ShapeDtypeStruct((BATCH, SEQ, SB_DIM), BF16),
        scratch_shapes=[
            pltpu.VMEM((2 * SB_TK, SB_TK), BF16), pltpu.VMEM((2 * SB_TK, SB_TK), BF16),
            pltpu.VMEM((2 * SB_TK, SB_TK), F32), pltpu.VMEM((2 * SB_TK, SB_TK), F32),
            pltpu.VMEM((2 * SB_TK, SB_TK), F32), pltpu.VMEM((2 * SB_TK, SB_TK), F32),
            pltpu.VMEM((SB_TK, 1), F32), pltpu.VMEM((SB_TK, SB_HEAD_DIM), F32),
            pltpu.VMEM((SEQ, SB_HEAD_DIM), F32), pltpu.VMEM((SEQ, 1), F32),
        ],
        compiler_params=pltpu.CompilerParams(dimension_semantics=("parallel", "parallel")),
        name="sb_attn",
    )(q, k, v)


def _post_kernel(x_ref, mix_ref, mod_ref, wo_ref, ng_ref, wg_ref, wu_ref, wd_ref, *rest):
    n_cast = (len(rest) - 1) // 2
    o_ref = rest[n_cast]
    b = pl.program_id(0)
    x = x_ref[0]
    y = jnp.dot(mix_ref[0], wo_ref[0], preferred_element_type=F32)
    x1 = x + _mod_row(mod_ref, 2, b) * y
    h = _norm_mod(x1, ng_ref[0], _mod_row(mod_ref, 4, b), _mod_row(mod_ref, 3, b)).astype(BF16)
    gate = jnp.dot(h, wg_ref[0], preferred_element_type=F32)
    up = jnp.dot(h, wu_ref[0], preferred_element_type=F32)
    act = ((gate * _sigmoid(gate)) * up).astype(BF16)
    ffn = jnp.dot(act, wd_ref[0], preferred_element_type=F32)
    o_ref[0] = x1 + _mod_row(mod_ref, 5, b) * ffn
    for src_ref, dst_ref in zip(rest[:n_cast], rest[n_cast + 1:]):
        dst_ref[...] = src_ref[...].astype(BF16)


def _cast_chunks(rows):
    steps = BATCH * (SEQ // POST_TM)
    n = steps
    while rows % (n * V7X_BF16_SUBLANES):
        n //= 2
    return n


def _post(x, mix, mod, l, w_out, norm_g, w_gate, w_up, w_down, cast_next=()):
    tm = POST_TM
    n_s = SEQ // tm
    layer = lambda b, s: (l, 0, 0)
    first = lambda b, s: (0, 0, 0)
    single = pl.Buffered(1)
    cast_in, cast_out, cast_shapes = [], [], []
    for w, li in cast_next:
        _, rows, cols = w.shape
        n = _cast_chunks(rows)
        chunk = lambda b, s, n=n: jnp.minimum(b * n_s + s, n - 1)
        cast_in.append(pl.BlockSpec((1, rows // n, cols), lambda b, s, li=li, chunk=chunk: (li, chunk(b, s), 0)))
        cast_out.append(pl.BlockSpec((1, rows // n, cols), lambda b, s, chunk=chunk: (0, chunk(b, s), 0)))
        cast_shapes.append(jax.ShapeDtypeStruct((1, rows, cols), BF16))
    outs = pl.pallas_call(
        _post_kernel,
        grid=(BATCH, n_s),
        in_specs=[
            pl.BlockSpec((1, tm, D_MODEL), lambda b, s: (b, s, 0)),
            pl.BlockSpec((1, tm, D_MODEL), lambda b, s: (b, s, 0)),
            pl.BlockSpec((1, 6, V7X_SUBLANES, D_MODEL), lambda b, s: (l, 0, 0, 0)),
            pl.BlockSpec((1, D_MODEL, D_MODEL), first, pipeline_mode=single),
            pl.BlockSpec((1, 1, D_MODEL), layer),
            pl.BlockSpec((1, D_MODEL, D_FF), first, pipeline_mode=single),
            pl.BlockSpec((1, D_MODEL, D_FF), first, pipeline_mode=single),
            pl.BlockSpec((1, D_FF, D_MODEL), first, pipeline_mode=single),
        ] + cast_in,
        out_specs=[pl.BlockSpec((1, tm, D_MODEL), lambda b, s: (b, s, 0))] + cast_out,
        out_shape=[jax.ShapeDtypeStruct((BATCH, SEQ, D_MODEL), F32)] + cast_shapes,
        compiler_params=pltpu.CompilerParams(
            dimension_semantics=("arbitrary", "arbitrary"),
            vmem_limit_bytes=58 * 1024 * 1024),
        name="post",
    )(x, mix, mod, w_out, norm_g, w_gate, w_up, w_down, *[w for w, _ in cast_next])
    return outs[0], outs[1:]


def kernel(x, c, ada_w, ada_b, norm_mix_g, norm_ffn_g, ev_w_in, ev_conv_w, ev_ret_norm_g, ev_w_out,
           od_w_qkv, od_q_norm_g, od_k_norm_g, od_w_out, ffn_w_gate, ffn_w_up, ffn_w_down):
    row3 = lambda g: g.reshape(g.shape[0], 1, g.shape[1])
    norm_mix_g, norm_ffn_g = row3(norm_mix_g), row3(norm_ffn_g)
    ev_ret_norm_g, od_q_norm_g, od_k_norm_g = row3(ev_ret_norm_g), row3(od_q_norm_g), row3(od_k_norm_g)
    ev_conv_w = jnp.pad(ev_conv_w, ((0, 0), (0, V7X_SUBLANES - CONV_WIDTH), (0, 0)))
    mod = _adaln(c, ada_w, ada_b)
    tables = _retention_tables()

    def f32_weights(l):
        j = l // 2
        mixer = (ev_w_in, ev_w_out) if l % 2 == 0 else (od_w_qkv, od_w_out)
        return [(mixer[0], j), (mixer[1], j), (ffn_w_gate, l), (ffn_w_up, l), (ffn_w_down, l)]

    weights = [w[i:i + 1].astype(BF16) for w, i in f32_weights(0)]
    for l in range(DEPTH):
        j = l // 2
        w_in, w_out, w_gate, w_up, w_down = weights
        if l % 2 == 0:
            mix = _even_mix(x, mod, l, j, norm_mix_g, w_in, ev_conv_w, ev_ret_norm_g, tables)
        else:
            q, k, v = _qkv(x, mod, l, j, norm_mix_g, w_in, od_q_norm_g, od_k_norm_g)
            mix = _sb_attention(q, k, v)
        cast_next = f32_weights(l + 1) if l + 1 < DEPTH else ()
        x, weights = _post(x, mix, mod, l, w_out, norm_ffn_g, w_gate, w_up, w_down, cast_next)
    return x
```

```python
import functools
import math

import jax
import jax.numpy as jnp
import numpy as np
from jax import lax
from jax.experimental import pallas as pl
from jax.experimental.pallas import tpu as pltpu

D_MODEL = 1024
BATCH = 4
SEQ = 4096
DEPTH = 4
CHUNK = 64
EPS = 1e-6
CONV_WIDTH = 3
CONV_DIM = 512
RET_HEADS = 4
RET_HEAD_DIM = 128
RET_DIM = 512
ROPE_THETA = 10000.0
SB_HEADS = 8
SB_HEAD_DIM = 128
SB_DIM = 1024
D_FF = 2816
EVEN_IN = 3 * CONV_DIM + 4 * RET_DIM

V7X_SUBLANES = 8
V7X_BF16_SUBLANES = 16
V7X_LANES = 128
V7X_MXU_DIM = 256
V7X_VMEM_BYTES = 64 * 1024 * 1024

LOG2E = 1.0 / math.log(2.0)

EVEN_TM = 1024
RET_TILE = V7X_MXU_DIM
QKV_TM = 1024
POST_TM = 512
SB_TK = V7X_MXU_DIM
MASKED_LOGIT = -1e30
SOFTPLUS_LINEAR = 64.0
F32_MIN_SUBNORMAL_LOG2 = -149.0
SB_EXIT_LOG2 = 160.0
assert SB_EXIT_LOG2 > -F32_MIN_SUBNORMAL_LOG2 + 1

F32 = jnp.float32
BF16 = jnp.bfloat16


def _sigmoid(x):
    return 1.0 / (1.0 + jnp.exp(-x))


def _norm_mod(x, gain, scale, shift):
    ms = jnp.mean(x * x, axis=-1, keepdims=True)
    return (x * lax.rsqrt(ms + EPS)) * (gain * (1.0 + scale)) + shift


def _mod_row(mod_ref, k, b):
    return mod_ref[0, k, pl.ds(b, 1), :]


def _adaln_kernel(c_ref, w_ref, b_ref, o_ref):
    c = c_ref[...]
    ca = c * _sigmoid(c)
    o_ref[0, 0] = jnp.dot(ca, w_ref[0], precision=lax.Precision.HIGHEST,
                          preferred_element_type=F32) + b_ref[0, 0]


def _adaln(c, ada_w, ada_b):
    c_pad = jnp.pad(c, ((0, V7X_SUBLANES - BATCH), (0, 0)))
    b4 = ada_b.reshape(DEPTH, 6, 1, D_MODEL)
    return pl.pallas_call(
        _adaln_kernel,
        grid=(DEPTH, 6),
        in_specs=[
            pl.BlockSpec((V7X_SUBLANES, D_MODEL), lambda l, k: (0, 0)),
            pl.BlockSpec((1, D_MODEL, D_MODEL), lambda l, k: (l, 0, k)),
            pl.BlockSpec((1, 1, 1, D_MODEL), lambda l, k: (l, k, 0, 0)),
        ],
        out_specs=pl.BlockSpec((1, 1, V7X_SUBLANES, D_MODEL), lambda l, k: (l, k, 0, 0)),
        out_shape=jax.ShapeDtypeStruct((DEPTH, 6, V7X_SUBLANES, D_MODEL), F32),
        compiler_params=pltpu.CompilerParams(dimension_semantics=("parallel", "parallel")),
        name="adaln",
    )(c_pad, ada_w, b4)


def _retention_tables():
    h = np.arange(RET_HEADS, dtype=np.float64)
    log_g = np.log1p(-np.exp2(-5.0 - h))
    idx = np.arange(RET_TILE, dtype=np.float64)
    diff = idx[:, None] - idx[None, :]
    same_or_earlier = (idx[None, :] // CHUNK) <= (idx[:, None] // CHUNK)
    k_scale = RET_HEAD_DIM ** -0.5
    dec = np.exp(np.abs(diff)[None] * log_g[:, None, None]) * same_or_earlier[None] * k_scale
    qdec = np.exp((idx + 1.0)[None, :] * log_g[:, None])
    kdec = np.exp((RET_TILE - 1.0 - idx)[None, :] * log_g[:, None]) * k_scale
    sdec = np.exp(RET_TILE * log_g)
    qdec = np.broadcast_to(qdec[:, :, None], (RET_HEADS, RET_TILE, V7X_LANES))
    kdec = np.broadcast_to(kdec[:, :, None], (RET_HEADS, RET_TILE, V7X_LANES))
    sdec = np.broadcast_to(sdec[:, None, None], (RET_HEADS, V7X_SUBLANES, V7X_LANES))
    inv_freq = 1.0 / (ROPE_THETA ** (np.arange(0, RET_HEAD_DIM, 2, dtype=np.float64) / RET_HEAD_DIM))
    ang = np.arange(SEQ, dtype=np.float64)[:, None] * inv_freq[None, :]
    cos2 = np.concatenate([np.cos(ang), np.cos(ang)], axis=-1)
    sin2 = np.concatenate([-np.sin(ang), np.sin(ang)], axis=-1)
    rot = np.stack([cos2, sin2])
    f = lambda a: jnp.asarray(np.ascontiguousarray(a), dtype=F32)
    return f(dec), f(qdec), f(kdec), f(sdec), f(rot)


def _even_kernel(x_ref, mod_ref, ng_ref, win_ref, cw_ref, rg_ref, rot_ref, dec_ref, qdec_ref,
                 kdec_ref, sdec_ref, mix_ref, state_ref, zbuf_ref):
    b = pl.program_id(0)
    si = pl.program_id(1)
    tm = EVEN_TM

    @pl.when(si == 0)
    def _():
        state_ref[...] = jnp.zeros_like(state_ref)
        zbuf_ref[0:V7X_SUBLANES, :] = jnp.zeros((V7X_SUBLANES, CONV_DIM), F32)

    q_off = 3 * CONV_DIM
    k_off = q_off + RET_DIM
    v_off = k_off + RET_DIM
    g_off = v_off + RET_DIM
    cw = cw_ref[0]
    sub_tiles = [slice(r0, r0 + RET_TILE) for r0 in range(0, tm, RET_TILE)]

    projs = []
    for rows in sub_tiles:
        h = _norm_mod(x_ref[0, rows, :], ng_ref[0], _mod_row(mod_ref, 1, b),
                      _mod_row(mod_ref, 0, b)).astype(BF16)
        projs.append(jnp.dot(h, win_ref[0], preferred_element_type=F32))

    for rows, proj in zip(sub_tiles, projs):
        r0 = rows.start
        zrows = lambda back: slice(V7X_SUBLANES + r0 - back, V7X_SUBLANES + r0 - back + RET_TILE)
        z = proj[:, CONV_DIM:2 * CONV_DIM] * proj[:, 2 * CONV_DIM:3 * CONV_DIM]
        zbuf_ref[zrows(0), :] = z
        y = cw[0:1, :] * zbuf_ref[zrows(2), :] + cw[1:2, :] * zbuf_ref[zrows(1), :] + cw[2:3, :] * z
        mix_ref[0, rows, 0:CONV_DIM] = (proj[:, 0:CONV_DIM] * y).astype(BF16)

        cos = rot_ref[0, rows, :]
        sin = rot_ref[1, rows, :]
        for hh in range(RET_HEADS):
            c0 = hh * RET_HEAD_DIM
            q = proj[:, q_off + c0:q_off + c0 + RET_HEAD_DIM]
            k = proj[:, k_off + c0:k_off + c0 + RET_HEAD_DIM]
            v = proj[:, v_off + c0:v_off + c0 + RET_HEAD_DIM].astype(BF16)
            g = proj[:, g_off + c0:g_off + c0 + RET_HEAD_DIM]
            qr = q * cos + pltpu.roll(q, RET_HEAD_DIM // 2, axis=1) * sin
            kr = k * cos + pltpu.roll(k, RET_HEAD_DIM // 2, axis=1) * sin
            s = lax.dot_general(qr.astype(BF16), kr.astype(BF16), (((1,), (1,)), ((), ())),
                                preferred_element_type=F32)
            s = s * dec_ref[hh]
            o = jnp.dot(s.astype(BF16), v, preferred_element_type=F32)
            st = state_ref[hh]
            o = o + jnp.dot((qr * qdec_ref[hh]).astype(BF16), st.astype(BF16),
                            preferred_element_type=F32)
            kd = (kr * kdec_ref[hh]).astype(BF16)
            kv = lax.dot_general(kd, v, (((0,), (0,)), ((), ())), preferred_element_type=F32)
            state_ref[hh] = st * sdec_ref[hh, 0:1, :] + kv
            ms = jnp.mean(o * o, axis=-1, keepdims=True)
            rn = (o * lax.rsqrt(ms + EPS)) * rg_ref[0, :, c0:c0 + RET_HEAD_DIM]
            out = (g * _sigmoid(g)) * rn
            mix_ref[0, rows, CONV_DIM + c0:CONV_DIM + c0 + RET_HEAD_DIM] = out.astype(BF16)
    zbuf_ref[0:V7X_SUBLANES, :] = zbuf_ref[tm:tm + V7X_SUBLANES, :]


def _even_mix(x, mod, l, j, norm_g, w_in, conv_w, ret_g, tables):
    dec, qdec, kdec, sdec, rot = tables
    tm = EVEN_TM
    const3 = lambda b, s: (0, 0, 0)
    layer = lambda b, s: (l, 0, 0)
    even = lambda b, s: (j, 0, 0)
    return pl.pallas_call(
        _even_kernel,
        grid=(BATCH, SEQ // tm),
        in_specs=[
            pl.BlockSpec((1, tm, D_MODEL), lambda b, s: (b, s, 0)),
            pl.BlockSpec((1, 6, V7X_SUBLANES, D_MODEL), lambda b, s: (l, 0, 0, 0)),
            pl.BlockSpec((1, 1, D_MODEL), layer),
            pl.BlockSpec((1, D_MODEL, EVEN_IN), const3, pipeline_mode=pl.Buffered(1)),
            pl.BlockSpec((1, V7X_SUBLANES, CONV_DIM), even),
            pl.BlockSpec((1, 1, RET_DIM), even),
            pl.BlockSpec((2, tm, RET_HEAD_DIM), lambda b, s: (0, s, 0)),
            pl.BlockSpec((RET_HEADS, RET_TILE, RET_TILE), const3),
            pl.BlockSpec((RET_HEADS, RET_TILE, V7X_LANES), const3),
            pl.BlockSpec((RET_HEADS, RET_TILE, V7X_LANES), const3),
            pl.BlockSpec((RET_HEADS, V7X_SUBLANES, V7X_LANES), const3),
        ],
        out_specs=pl.BlockSpec((1, tm, D_MODEL), lambda b, s: (b, s, 0)),
        out_shape=jax.ShapeDtypeStruct((BATCH, SEQ, D_MODEL), BF16),
        scratch_shapes=[
            pltpu.VMEM((RET_HEADS, RET_HEAD_DIM, RET_HEAD_DIM), F32),
            pltpu.VMEM((tm + V7X_SUBLANES, CONV_DIM), F32),
        ],
        compiler_params=pltpu.CompilerParams(
            dimension_semantics=("arbitrary", "arbitrary"),
            vmem_limit_bytes=56 * 1024 * 1024),
        name="even_mix",
    )(x, mod, norm_g, w_in, conv_w, ret_g, rot, dec, qdec, kdec, sdec)


def _qkv_kernel(x_ref, mod_ref, ng_ref, w_ref, qg_ref, kg_ref, q_ref, k_ref, v_ref):
    b = pl.program_id(0)
    x = x_ref[0]
    h = _norm_mod(x, ng_ref[0], _mod_row(mod_ref, 1, b), _mod_row(mod_ref, 0, b)).astype(BF16)
    qkv = jnp.dot(h, w_ref[0], preferred_element_type=F32)
    qg = qg_ref[0] * (SB_HEAD_DIM ** -0.5 * LOG2E)
    kg = kg_ref[0]
    for hh in range(SB_HEADS):
        c0 = hh * SB_HEAD_DIM
        q = qkv[:, c0:c0 + SB_HEAD_DIM]
        k = qkv[:, SB_DIM + c0:SB_DIM + c0 + SB_HEAD_DIM]
        v = qkv[:, 2 * SB_DIM + c0:2 * SB_DIM + c0 + SB_HEAD_DIM]
        qn = (q * lax.rsqrt(jnp.mean(q * q, axis=-1, keepdims=True) + EPS)) * qg
        kn = (k * lax.rsqrt(jnp.mean(k * k, axis=-1, keepdims=True) + EPS)) * kg
        q_ref[0, hh] = qn.astype(BF16)
        k_ref[0, hh] = kn.astype(BF16)
        v_ref[0, hh] = v.astype(BF16)


def _qkv(x, mod, l, j, norm_g, w_qkv, q_g, k_g):
    tm = QKV_TM
    layer = lambda b, s: (l, 0, 0)
    odd = lambda b, s: (j, 0, 0)
    head_spec = pl.BlockSpec((1, SB_HEADS, tm, SB_HEAD_DIM), lambda b, s: (b, 0, s, 0))
    head_shape = jax.ShapeDtypeStruct((BATCH, SB_HEADS, SEQ, SB_HEAD_DIM), BF16)
    return pl.pallas_call(
        _qkv_kernel,
        grid=(BATCH, SEQ // tm),
        in_specs=[
            pl.BlockSpec((1, tm, D_MODEL), lambda b, s: (b, s, 0)),
            pl.BlockSpec((1, 6, V7X_SUBLANES, D_MODEL), lambda b, s: (l, 0, 0, 0)),
            pl.BlockSpec((1, 1, D_MODEL), layer),
            pl.BlockSpec((1, D_MODEL, 3 * SB_DIM), lambda b, s: (0, 0, 0),
                         pipeline_mode=pl.Buffered(1)),
            pl.BlockSpec((1, 1, SB_HEAD_DIM), odd),
            pl.BlockSpec((1, 1, SB_HEAD_DIM), odd),
        ],
        out_specs=[head_spec, head_spec, head_spec],
        out_shape=[head_shape, head_shape, head_shape],
        compiler_params=pltpu.CompilerParams(
            dimension_semantics=("parallel", "parallel"),
            vmem_limit_bytes=56 * 1024 * 1024),
        name="qkv",
    )(x, mod, norm_g, w_qkv, q_g, k_g)


def _softplus2(z):
    return jnp.maximum(z, jnp.log(1.0 + jnp.exp2(jnp.minimum(z, SOFTPLUS_LINEAR))) * LOG2E)


def _sb_kernel(q_ref, k_ref, v_ref, o_ref, spb0, spb1, lb0, lb1, a0, a1, rc_ref, ac_ref,
               acc_all, r_all):
    row = lax.broadcasted_iota(jnp.int32, (SB_TK, SB_TK), 0)
    col = lax.broadcasted_iota(jnp.int32, (SB_TK, SB_TK), 1)
    later = (row > col).astype(BF16)
    causal = col < row
    sets = ((spb0, lb0), (spb1, lb1))
    abuf = (a0, a1)
    nblk = SEQ // SB_TK

    def blk(ref, j, rows=SB_TK):
        return ref[0, 0, pl.ds(pl.multiple_of(j * SB_TK, SB_TK), rows), :]

    def qk(q, j):
        return lax.dot_general(q, blk(k_ref, j), (((1,), (1,)), ((), ())),
                               preferred_element_type=F32)

    def stage_a(j, i, m):
        spb_ref, lb_ref = sets[i]
        z = qk(blk(q_ref, j, m), j)
        z_top = jnp.where(causal, z[:SB_TK], MASKED_LOGIT)
        sp_top = _softplus2(z_top)
        spb_ref[0:SB_TK] = sp_top.astype(BF16)
        lb_ref[0:SB_TK] = z_top - sp_top
        if m > SB_TK:
            sp = _softplus2(z[SB_TK:])
            spb_ref[SB_TK:m] = sp.astype(BF16)
            lb_ref[SB_TK:m] = z[SB_TK:] - sp

    def stage_b(j, i, m):
        spb_ref, lb_ref = sets[i]
        cs = jnp.dot(spb_ref[0:m], later, preferred_element_type=F32)
        rowsum = cs[:, 0:1] + spb_ref[0:m, 0:1].astype(F32)
        abuf[i][0:SB_TK] = lb_ref[0:SB_TK] - cs[:SB_TK]
        if m > SB_TK:
            r_old = rc_ref[...]
            abuf[i][SB_TK:m] = (lb_ref[SB_TK:m] - cs[SB_TK:]) - r_old
            r_all[pl.ds(pl.multiple_of((j + 1) * SB_TK, SB_TK), SB_TK), :] = r_old + rowsum[SB_TK:]
        rc_ref[...] = rowsum[:SB_TK]

    def stage_c(j, i, m):
        w = jnp.exp2(abuf[i][0:m]).astype(BF16)
        pv = jnp.dot(w, blk(v_ref, j), preferred_element_type=F32)
        if m > SB_TK:
            acc_all[pl.ds(pl.multiple_of((j + 1) * SB_TK, SB_TK), SB_TK), :] = ac_ref[...] + pv[SB_TK:]
        ac_ref[...] = pv[:SB_TK]

    def tick(tau, par, a=True, b=True, c=True):
        rows = lambda t: SB_TK if (isinstance(t, int) and t == 0) else 2 * SB_TK
        if a:
            stage_a(nblk - 1 - tau, par, rows(tau))
        if b:
            stage_b(nblk - tau, 1 - par, rows(tau - 1))
        if c:
            stage_c(nblk + 1 - tau, par, rows(tau - 2))

    for tau in range(nblk + 2):
        tick(tau, tau % 2, a=tau < nblk, b=1 <= tau <= nblk, c=tau >= 2)
    acc_all[0:SB_TK] = ac_ref[...]

    r_all[0:2 * SB_TK] = jnp.full((2 * SB_TK, 1), 2 * SB_EXIT_LOG2, F32)

    @pl.when(jnp.min(r_all[...]) <= SB_EXIT_LOG2)
    def _():
        def per_block(m, carry):
            rows = pl.ds(pl.multiple_of(m * SB_TK, SB_TK), SB_TK)

            def step(c2):
                j, _ = c2
                z = qk(blk(q_ref, m), j)
                sp = _softplus2(z)
                spb = sp.astype(BF16)
                cs = jnp.dot(spb, later, preferred_element_type=F32)
                r = r_all[rows, :]
                w = jnp.exp2(((z - sp) - cs) - r).astype(BF16)
                acc_all[rows, :] += jnp.dot(w, blk(v_ref, j), preferred_element_type=F32)
                r = r + (cs[:, 0:1] + spb[:, 0:1].astype(F32))
                r_all[rows, :] = r
                return j - 1, jnp.min(r)

            def more(c2):
                j, r_min = c2
                return jnp.logical_and(j >= 0, r_min <= SB_EXIT_LOG2)

            lax.while_loop(more, step, (m - 2, jnp.min(r_all[rows, :])))
            return carry

        lax.fori_loop(2, nblk, per_block, 0)

    o_ref[0] = acc_all[...].astype(BF16)


def _sb_attention(q, k, v):
    head_spec = pl.BlockSpec((1, 1, SEQ, SB_HEAD_DIM), lambda b, h: (b, h, 0, 0))
    return pl.pallas_call(
        _sb_kernel,
        grid=(BATCH, SB_HEADS),
        in_specs=[head_spec, head_spec, head_spec],
        out_specs=pl.BlockSpec((1, SEQ, SB_HEAD_DIM), lambda b, h: (b, 0, h)),
        out_shape=jax.ShapeDtypeStruct((BATCH, SEQ, SB_DIM), BF16),
        scratch_shapes=[
            pltpu.VMEM((2 * SB_TK, SB_TK), BF16), pltpu.VMEM((2 * SB_TK, SB_TK), BF16),
            pltpu.VMEM((2 * SB_TK, SB_TK), F32), pltpu.VMEM((2 * SB_TK, SB_TK), F32),
            pltpu.VMEM((2 * SB_TK, SB_TK), F32), pltpu.VMEM((2 * SB_TK, SB_TK), F32),
            pltpu.VMEM((SB_TK, 1), F32), pltpu.VMEM((SB_TK, SB_HEAD_DIM), F32),
            pltpu.VMEM((SEQ, SB_HEAD_DIM), F32), pltpu.VMEM((SEQ, 1), F32),
        ],
        compiler_params=pltpu.CompilerParams(dimension_semantics=("parallel", "parallel")),
        name="sb_attn",
    )(q, k, v)


def _post_kernel(x_ref, mix_ref, mod_ref, wo_ref, ng_ref, wg_ref, wu_ref, wd_ref, *rest):
    n_cast = (len(rest) - 1) // 2
    o_ref = rest[n_cast]
    b = pl.program_id(0)
    x = x_ref[0]
    y = jnp.dot(mix_ref[0], wo_ref[0], preferred_element_type=F32)
    x1 = x + _mod_row(mod_ref, 2, b) * y
    h = _norm_mod(x1, ng_ref[0], _mod_row(mod_ref, 4, b), _mod_row(mod_ref, 3, b)).astype(BF16)
    gate = jnp.dot(h, wg_ref[0], preferred_element_type=F32)
    up = jnp.dot(h, wu_ref[0], preferred_element_type=F32)
    act = ((gate * _sigmoid(gate)) * up).astype(BF16)
    ffn = jnp.dot(act, wd_ref[0], preferred_element_type=F32)
    o_ref[0] = x1 + _mod_row(mod_ref, 5, b) * ffn
    for src_ref, dst_ref in zip(rest[:n_cast], rest[n_cast + 1:]):
        dst_ref[...] = src_ref[...].astype(BF16)


def _cast_chunks(rows):
    steps = BATCH * (SEQ // POST_TM)
    n = steps
    while rows % (n * V7X_BF16_SUBLANES):
        n //= 2
    return n


def _post(x, mix, mod, l, w_out, norm_g, w_gate, w_up, w_down, cast_next=()):
    tm = POST_TM
    n_s = SEQ // tm
    layer = lambda b, s: (l, 0, 0)
    first = lambda b, s: (0, 0, 0)
    single = pl.Buffered(1)
    cast_in, cast_out, cast_shapes = [], [], []
    for w, li in cast_next:
        _, rows, cols = w.shape
        n = _cast_chunks(rows)
        chunk = lambda b, s, n=n: jnp.minimum(b * n_s + s, n - 1)
        cast_in.append(pl.BlockSpec((1, rows // n, cols), lambda b, s, li=li, chunk=chunk: (li, chunk(b, s), 0)))
        cast_out.append(pl.BlockSpec((1, rows // n, cols), lambda b, s, chunk=chunk: (0, chunk(b, s), 0)))
        cast_shapes.append(jax.ShapeDtypeStruct((1, rows, cols), BF16))
    outs = pl.pallas_call(
        _post_kernel,
        grid=(BATCH, n_s),
        in_specs=[
            pl.BlockSpec((1, tm, D_MODEL), lambda b, s: (b, s, 0)),
            pl.BlockSpec((1, tm, D_MODEL), lambda b, s: (b, s, 0)),
            pl.BlockSpec((1, 6, V7X_SUBLANES, D_MODEL), lambda b, s: (l, 0, 0, 0)),
            pl.BlockSpec((1, D_MODEL, D_MODEL), first, pipeline_mode=single),
            pl.BlockSpec((1, 1, D_MODEL), layer),
            pl.BlockSpec((1, D_MODEL, D_FF), first, pipeline_mode=single),
            pl.BlockSpec((1, D_MODEL, D_FF), first, pipeline_mode=single),
            pl.BlockSpec((1, D_FF, D_MODEL), first, pipeline_mode=single),
        ] + cast_in,
        out_specs=[pl.BlockSpec((1, tm, D_MODEL), lambda b, s: (b, s, 0))] + cast_out,
        out_shape=[jax.ShapeDtypeStruct((BATCH, SEQ, D_MODEL), F32)] + cast_shapes,
        compiler_params=pltpu.CompilerParams(
            dimension_semantics=("arbitrary", "arbitrary"),
            vmem_limit_bytes=58 * 1024 * 1024),
        name="post",
    )(x, mix, mod, w_out, norm_g, w_gate, w_up, w_down, *[w for w, _ in cast_next])
    return outs[0], outs[1:]


def kernel(x, c, ada_w, ada_b, norm_mix_g, norm_ffn_g, ev_w_in, ev_conv_w, ev_ret_norm_g, ev_w_out,
           od_w_qkv, od_q_norm_g, od_k_norm_g, od_w_out, ffn_w_gate, ffn_w_up, ffn_w_down):
    row3 = lambda g: g.reshape(g.shape[0], 1, g.shape[1])
    norm_mix_g, norm_ffn_g = row3(norm_mix_g), row3(norm_ffn_g)
    ev_ret_norm_g, od_q_norm_g, od_k_norm_g = row3(ev_ret_norm_g), row3(od_q_norm_g), row3(od_k_norm_g)
    ev_conv_w = jnp.pad(ev_conv_w, ((0, 0), (0, V7X_SUBLANES - CONV_WIDTH), (0, 0)))
    mod = _adaln(c, ada_w, ada_b)
    tables = _retention_tables()

    def f32_weights(l):
        j = l // 2
        mixer = (ev_w_in, ev_w_out) if l % 2 == 0 else (od_w_qkv, od_w_out)
        return [(mixer[0], j), (mixer[1], j), (ffn_w_gate, l), (ffn_w_up, l), (ffn_w_down, l)]

    weights = [w[i:i + 1].astype(BF16) for w, i in f32_weights(0)]
    for l in range(DEPTH):
        j = l // 2
        w_in, w_out, w_gate, w_up, w_down = weights
        if l % 2 == 0:
            mix = _even_mix(x, mod, l, j, norm_mix_g, w_in, ev_conv_w, ev_ret_norm_g, tables)
        else:
            q, k, v = _qkv(x, mod, l, j, norm_mix_g, w_in, od_q_norm_g, od_k_norm_g)
            mix = _sb_attention(q, k, v)
        cast_next = f32_weights(l + 1) if l + 1 < DEPTH else ()
        x, weights = _post(x, mix, mod, l, w_out, norm_ffn_g, w_gate, w_up, w_down, cast_next)
    return x
```

```python
import functools
import math

import jax
import jax.numpy as jnp
import numpy as np
from jax import lax
from jax.experimental import pallas as pl
from jax.experimental.pallas import tpu as pltpu

D_MODEL = 1024
BATCH = 4
SEQ = 4096
DEPTH = 4
CHUNK = 64
EPS = 1e-6
CONV_WIDTH = 3
CONV_DIM = 512
RET_HEADS = 4
RET_HEAD_DIM = 128
RET_DIM = 512
ROPE_THETA = 10000.0
SB_HEADS = 8
SB_HEAD_DIM = 128
SB_DIM = 1024
D_FF = 2816
EVEN_IN = 3 * CONV_DIM + 4 * RET_DIM

V7X_SUBLANES = 8
V7X_BF16_SUBLANES = 16
V7X_LANES = 128
V7X_MXU_DIM = 256
V7X_VMEM_BYTES = 64 * 1024 * 1024

LOG2E = 1.0 / math.log(2.0)

EVEN_TM = 1024
RET_TILE = V7X_MXU_DIM
QKV_TM = 1024
POST_TM = 512
SB_TK = V7X_MXU_DIM
MASKED_LOGIT = -1e30
SOFTPLUS_LINEAR = 64.0
F32_MIN_SUBNORMAL_LOG2 = -149.0
SB_EXIT_LOG2 = 160.0
assert SB_EXIT_LOG2 > -F32_MIN_SUBNORMAL_LOG2 + 1

F32 = jnp.float32
BF16 = jnp.bfloat16


def _sigmoid(x):
    return 1.0 / (1.0 + jnp.exp(-x))


def _norm_mod(x, gain, scale, shift):
    ms = jnp.mean(x * x, axis=-1, keepdims=True)
    return (x * lax.rsqrt(ms + EPS)) * (gain * (1.0 + scale)) + shift


def _mod_row(mod_ref, k, b):
    return mod_ref[0, k, pl.ds(b, 1), :]


def _cast_specs(cast, step_of, n_steps):
    in_specs, out_specs, shapes = [], [], []
    for w, li in cast:
        _, rows, cols = w.shape
        n = 1
        while 2 * n <= n_steps and rows % (2 * n * V7X_BF16_SUBLANES) == 0:
            n *= 2
        chunk = lambda *ids, n=n: jnp.minimum(step_of(*ids), n - 1)
        in_specs.append(pl.BlockSpec((1, rows // n, cols),
                                     lambda *ids, li=li, chunk=chunk: (li, chunk(*ids), 0)))
        out_specs.append(pl.BlockSpec((1, rows // n, cols),
                                      lambda *ids, chunk=chunk: (0, chunk(*ids), 0)))
        shapes.append(jax.ShapeDtypeStruct((1, rows, cols), BF16))
    return in_specs, out_specs, shapes


def _cast_blocks(src_refs, dst_refs):
    for src_ref, dst_ref in zip(src_refs, dst_refs):
        dst_ref[...] = src_ref[...].astype(BF16)


def _adaln_kernel(c_ref, w_ref, b_ref, *rest):
    n_cast = (len(rest) - 1) // 2
    o_ref = rest[n_cast]
    c = c_ref[...]
    ca = c * _sigmoid(c)
    o_ref[0, 0] = jnp.dot(ca, w_ref[0], precision=lax.Precision.HIGHEST,
                          preferred_element_type=F32) + b_ref[0, 0]
    _cast_blocks(rest[:n_cast], rest[n_cast + 1:])


def _adaln(c, ada_w, ada_b, cast_first=()):
    c_pad = jnp.pad(c, ((0, V7X_SUBLANES - BATCH), (0, 0)))
    b4 = ada_b.reshape(DEPTH, 6, 1, D_MODEL)
    cast_in, cast_out, cast_shapes = _cast_specs(cast_first, lambda l, k: l * 6 + k, DEPTH * 6)
    outs = pl.pallas_call(
        _adaln_kernel,
        grid=(DEPTH, 6),
        in_specs=[
            pl.BlockSpec((V7X_SUBLANES, D_MODEL), lambda l, k: (0, 0)),
            pl.BlockSpec((1, D_MODEL, D_MODEL), lambda l, k: (l, 0, k)),
            pl.BlockSpec((1, 1, 1, D_MODEL), lambda l, k: (l, k, 0, 0)),
        ] + cast_in,
        out_specs=[pl.BlockSpec((1, 1, V7X_SUBLANES, D_MODEL), lambda l, k: (l, k, 0, 0))] + cast_out,
        out_shape=[jax.ShapeDtypeStruct((DEPTH, 6, V7X_SUBLANES, D_MODEL), F32)] + cast_shapes,
        compiler_params=pltpu.CompilerParams(dimension_semantics=("arbitrary", "arbitrary")),
        name="adaln",
    )(c_pad, ada_w, b4, *[w for w, _ in cast_first])
    return outs[0], outs[1:]


def _retention_tables():
    h = np.arange(RET_HEADS, dtype=np.float64)
    log_g = np.log1p(-np.exp2(-5.0 - h))
    idx = np.arange(RET_TILE, dtype=np.float64)
    diff = idx[:, None] - idx[None, :]
    same_or_earlier = (idx[None, :] // CHUNK) <= (idx[:, None] // CHUNK)
    k_scale = RET_HEAD_DIM ** -0.5
    dec = np.exp(np.abs(diff)[None] * log_g[:, None, None]) * same_or_earlier[None] * k_scale
    qdec = np.exp((idx + 1.0)[None, :] * log_g[:, None])
    kdec = np.exp((RET_TILE - 1.0 - idx)[None, :] * log_g[:, None]) * k_scale
    sdec = np.exp(RET_TILE * log_g)
    qdec = np.broadcast_to(qdec[:, :, None], (RET_HEADS, RET_TILE, V7X_LANES))
    kdec = np.broadcast_to(kdec[:, :, None], (RET_HEADS, RET_TILE, V7X_LANES))
    sdec = np.broadcast_to(sdec[:, None, None], (RET_HEADS, V7X_SUBLANES, V7X_LANES))
    inv_freq = 1.0 / (ROPE_THETA ** (np.arange(0, RET_HEAD_DIM, 2, dtype=np.float64) / RET_HEAD_DIM))
    ang = np.arange(SEQ, dtype=np.float64)[:, None] * inv_freq[None, :]
    cos2 = np.concatenate([np.cos(ang), np.cos(ang)], axis=-1)
    sin2 = np.concatenate([-np.sin(ang), np.sin(ang)], axis=-1)
    rot = np.stack([cos2, sin2])
    f = lambda a: jnp.asarray(np.ascontiguousarray(a), dtype=F32)
    return f(dec), f(qdec), f(kdec), f(sdec), f(rot)


def _even_kernel(x_ref, mod_ref, ng_ref, win_ref, cw_ref, rg_ref, rot_ref, dec_ref, qdec_ref,
                 kdec_ref, sdec_ref, mix_ref, state_ref, zbuf_ref):
    b = pl.program_id(0)
    si = pl.program_id(1)
    tm = EVEN_TM

    @pl.when(si == 0)
    def _():
        state_ref[...] = jnp.zeros_like(state_ref)
        zbuf_ref[0:V7X_SUBLANES, :] = jnp.zeros((V7X_SUBLANES, CONV_DIM), F32)

    q_off = 3 * CONV_DIM
    k_off = q_off + RET_DIM
    v_off = k_off + RET_DIM
    g_off = v_off + RET_DIM
    cw = cw_ref[0]
    sub_tiles = [slice(r0, r0 + RET_TILE) for r0 in range(0, tm, RET_TILE)]

    projs = []
    for rows in sub_tiles:
        h = _norm_mod(x_ref[0, rows, :], ng_ref[0], _mod_row(mod_ref, 1, b),
                      _mod_row(mod_ref, 0, b)).astype(BF16)
        projs.append(jnp.dot(h, win_ref[0], preferred_element_type=F32))

    for rows, proj in zip(sub_tiles, projs):
        r0 = rows.start
        zrows = lambda back: slice(V7X_SUBLANES + r0 - back, V7X_SUBLANES + r0 - back + RET_TILE)
        z = proj[:, CONV_DIM:2 * CONV_DIM] * proj[:, 2 * CONV_DIM:3 * CONV_DIM]
        zbuf_ref[zrows(0), :] = z
        y = cw[0:1, :] * zbuf_ref[zrows(2), :] + cw[1:2, :] * zbuf_ref[zrows(1), :] + cw[2:3, :] * z
        mix_ref[0, rows, 0:CONV_DIM] = (proj[:, 0:CONV_DIM] * y).astype(BF16)

        cos = rot_ref[0, rows, :]
        sin = rot_ref[1, rows, :]
        for hh in range(RET_HEADS):
            c0 = hh * RET_HEAD_DIM
            q = proj[:, q_off + c0:q_off + c0 + RET_HEAD_DIM]
            k = proj[:, k_off + c0:k_off + c0 + RET_HEAD_DIM]
            v = proj[:, v_off + c0:v_off + c0 + RET_HEAD_DIM].astype(BF16)
            g = proj[:, g_off + c0:g_off + c0 + RET_HEAD_DIM]
            qr = q * cos + pltpu.roll(q, RET_HEAD_DIM // 2, axis=1) * sin
            kr = k * cos + pltpu.roll(k, RET_HEAD_DIM // 2, axis=1) * sin
            s = lax.dot_general(qr.astype(BF16), kr.astype(BF16), (((1,), (1,)), ((), ())),
                                preferred_element_type=F32)
            s = s * dec_ref[hh]
            o = jnp.dot(s.astype(BF16), v, preferred_element_type=F32)
            st = state_ref[hh]
            o = o + jnp.dot((qr * qdec_ref[hh]).astype(BF16), st.astype(BF16),
                            preferred_element_type=F32)
            kd = (kr * kdec_ref[hh]).astype(BF16)
            kv = lax.dot_general(kd, v, (((0,), (0,)), ((), ())), preferred_element_type=F32)
            state_ref[hh] = st * sdec_ref[hh, 0:1, :] + kv
            ms = jnp.mean(o * o, axis=-1, keepdims=True)
            rn = (o * lax.rsqrt(ms + EPS)) * rg_ref[0, :, c0:c0 + RET_HEAD_DIM]
            out = (g * _sigmoid(g)) * rn
            mix_ref[0, rows, CONV_DIM + c0:CONV_DIM + c0 + RET_HEAD_DIM] = out.astype(BF16)
    zbuf_ref[0:V7X_SUBLANES, :] = zbuf_ref[tm:tm + V7X_SUBLANES, :]


def _even_mix(x, mod, l, j, norm_g, w_in, conv_w, ret_g, tables):
    dec, qdec, kdec, sdec, rot = tables
    tm = EVEN_TM
    const3 = lambda b, s: (0, 0, 0)
    layer = lambda b, s: (l, 0, 0)
    even = lambda b, s: (j, 0, 0)
    return pl.pallas_call(
        _even_kernel,
        grid=(BATCH, SEQ // tm),
        in_specs=[
            pl.BlockSpec((1, tm, D_MODEL), lambda b, s: (b, s, 0)),
            pl.BlockSpec((1, 6, V7X_SUBLANES, D_MODEL), lambda b, s: (l, 0, 0, 0)),
            pl.BlockSpec((1, 1, D_MODEL), layer),
            pl.BlockSpec((1, D_MODEL, EVEN_IN), const3, pipeline_mode=pl.Buffered(1)),
            pl.BlockSpec((1, V7X_SUBLANES, CONV_DIM), even),
            pl.BlockSpec((1, 1, RET_DIM), even),
            pl.BlockSpec((2, tm, RET_HEAD_DIM), lambda b, s: (0, s, 0)),
            pl.BlockSpec((RET_HEADS, RET_TILE, RET_TILE), const3),
            pl.BlockSpec((RET_HEADS, RET_TILE, V7X_LANES), const3),
            pl.BlockSpec((RET_HEADS, RET_TILE, V7X_LANES), const3),
            pl.BlockSpec((RET_HEADS, V7X_SUBLANES, V7X_LANES), const3),
        ],
        out_specs=pl.BlockSpec((1, tm, D_MODEL), lambda b, s: (b, s, 0)),
        out_shape=jax.ShapeDtypeStruct((BATCH, SEQ, D_MODEL), BF16),
        scratch_shapes=[
            pltpu.VMEM((RET_HEADS, RET_HEAD_DIM, RET_HEAD_DIM), F32),
            pltpu.VMEM((tm + V7X_SUBLANES, CONV_DIM), F32),
        ],
        compiler_params=pltpu.CompilerParams(
            dimension_semantics=("arbitrary", "arbitrary"),
            vmem_limit_bytes=56 * 1024 * 1024),
        name="even_mix",
    )(x, mod, norm_g, w_in, conv_w, ret_g, rot, dec, qdec, kdec, sdec)


def _qkv_kernel(x_ref, mod_ref, ng_ref, w_ref, qg_ref, kg_ref, q_ref, k_ref, v_ref):
    b = pl.program_id(0)
    x = x_ref[0]
    h = _norm_mod(x, ng_ref[0], _mod_row(mod_ref, 1, b), _mod_row(mod_ref, 0, b)).astype(BF16)
    qkv = jnp.dot(h, w_ref[0], preferred_element_type=F32)
    qg = qg_ref[0] * (SB_HEAD_DIM ** -0.5 * LOG2E)
    kg = kg_ref[0]
    for hh in range(SB_HEADS):
        c0 = hh * SB_HEAD_DIM
        q = qkv[:, c0:c0 + SB_HEAD_DIM]
        k = qkv[:, SB_DIM + c0:SB_DIM + c0 + SB_HEAD_DIM]
        v = qkv[:, 2 * SB_DIM + c0:2 * SB_DIM + c0 + SB_HEAD_DIM]
        qn = (q * lax.rsqrt(jnp.mean(q * q, axis=-1, keepdims=True) + EPS)) * qg
        kn = (k * lax.rsqrt(jnp.mean(k * k, axis=-1, keepdims=True) + EPS)) * kg
        q_ref[0, hh] = qn.astype(BF16)
        k_ref[0, hh] = kn.astype(BF16)
        v_ref[0, hh] = v.astype(BF16)


def _qkv(x, mod, l, j, norm_g, w_qkv, q_g, k_g):
    tm = QKV_TM
    layer = lambda b, s: (l, 0, 0)
    odd = lambda b, s: (j, 0, 0)
    head_spec = pl.BlockSpec((1, SB_HEADS, tm, SB_HEAD_DIM), lambda b, s: (b, 0, s, 0))
    head_shape = jax.ShapeDtypeStruct((BATCH, SB_HEADS, SEQ, SB_HEAD_DIM), BF16)
    return pl.pallas_call(
        _qkv_kernel,
        grid=(BATCH, SEQ // tm),
        in_specs=[
            pl.BlockSpec((1, tm, D_MODEL), lambda b, s: (b, s, 0)),
            pl.BlockSpec((1, 6, V7X_SUBLANES, D_MODEL), lambda b, s: (l, 0, 0, 0)),
            pl.BlockSpec((1, 1, D_MODEL), layer),
            pl.BlockSpec((1, D_MODEL, 3 * SB_DIM), lambda b, s: (0, 0, 0),
                         pipeline_mode=pl.Buffered(1)),
            pl.BlockSpec((1, 1, SB_HEAD_DIM), odd),
            pl.BlockSpec((1, 1, SB_HEAD_DIM), odd),
        ],
        out_specs=[head_spec, head_spec, head_spec],
        out_shape=[head_shape, head_shape, head_shape],
        compiler_params=pltpu.CompilerParams(
            dimension_semantics=("parallel", "parallel"),
            vmem_limit_bytes=56 * 1024 * 1024),
        name="qkv",
    )(x, mod, norm_g, w_qkv, q_g, k_g)


def _softplus2(z):
    return jnp.maximum(z, jnp.log(1.0 + jnp.exp2(jnp.minimum(z, SOFTPLUS_LINEAR))) * LOG2E)


def _sb_kernel(q_ref, k_ref, v_ref, o_ref, spb0, spb1, lb0, lb1, a0, a1, rc_ref, ac_ref,
               acc_all, r_all):
    row = lax.broadcasted_iota(jnp.int32, (SB_TK, SB_TK), 0)
    col = lax.broadcasted_iota(jnp.int32, (SB_TK, SB_TK), 1)
    later = (row > col).astype(BF16)
    causal = col < row
    sets = ((spb0, lb0), (spb1, lb1))
    abuf = (a0, a1)
    nblk = SEQ // SB_TK

    def blk(ref, j, rows=SB_TK):
        return ref[0, 0, pl.ds(pl.multiple_of(j * SB_TK, SB_TK), rows), :]

    def qk(q, j):
        return lax.dot_general(q, blk(k_ref, j), (((1,), (1,)), ((), ())),
                               preferred_element_type=F32)

    def stage_a(j, i, m):
        spb_ref, lb_ref = sets[i]
        z = qk(blk(q_ref, j, m), j)
        z_top = jnp.where(causal, z[:SB_TK], MASKED_LOGIT)
        sp_top = _softplus2(z_top)
        spb_ref[0:SB_TK] = sp_top.astype(BF16)
        lb_ref[0:SB_TK] = z_top - sp_top
        if m > SB_TK:
            sp = _softplus2(z[SB_TK:])
            spb_ref[SB_TK:m] = sp.astype(BF16)
            lb_ref[SB_TK:m] = z[SB_TK:] - sp

    def stage_b(j, i, m):
        spb_ref, lb_ref = sets[i]
        cs = jnp.dot(spb_ref[0:m], later, preferred_element_type=F32)
        rowsum = cs[:, 0:1] + spb_ref[0:m, 0:1].astype(F32)
        abuf[i][0:SB_TK] = lb_ref[0:SB_TK] - cs[:SB_TK]
        if m > SB_TK:
            r_old = rc_ref[...]
            abuf[i][SB_TK:m] = (lb_ref[SB_TK:m] - cs[SB_TK:]) - r_old
            r_all[pl.ds(pl.multiple_of((j + 1) * SB_TK, SB_TK), SB_TK), :] = r_old + rowsum[SB_TK:]
        rc_ref[...] = rowsum[:SB_TK]

    def stage_c(j, i, m):
        w = jnp.exp2(abuf[i][0:m]).astype(BF16)
        pv = jnp.dot(w, blk(v_ref, j), preferred_element_type=F32)
        if m > SB_TK:
            acc_all[pl.ds(pl.multiple_of((j + 1) * SB_TK, SB_TK), SB_TK), :] = ac_ref[...] + pv[SB_TK:]
        ac_ref[...] = pv[:SB_TK]

    def tick(tau, par, a=True, b=True, c=True):
        rows = lambda t: SB_TK if (isinstance(t, int) and t == 0) else 2 * SB_TK
        if a:
            stage_a(nblk - 1 - tau, par, rows(tau))
        if b:
            stage_b(nblk - tau, 1 - par, rows(tau - 1))
        if c:
            stage_c(nblk + 1 - tau, par, rows(tau - 2))

    for tau in range(nblk + 2):
        tick(tau, tau % 2, a=tau < nblk, b=1 <= tau <= nblk, c=tau >= 2)
    acc_all[0:SB_TK] = ac_ref[...]

    r_all[0:2 * SB_TK] = jnp.full((2 * SB_TK, 1), 2 * SB_EXIT_LOG2, F32)

    @pl.when(jnp.min(r_all[...]) <= SB_EXIT_LOG2)
    def _():
        def per_block(m, carry):
            rows = pl.ds(pl.multiple_of(m * SB_TK, SB_TK), SB_TK)

            def step(c2):
                j, _ = c2
                z = qk(blk(q_ref, m), j)
                sp = _softplus2(z)
                spb = sp.astype(BF16)
                cs = jnp.dot(spb, later, preferred_element_type=F32)
                r = r_all[rows, :]
                w = jnp.exp2(((z - sp) - cs) - r).astype(BF16)
                acc_all[rows, :] += jnp.dot(w, blk(v_ref, j), preferred_element_type=F32)
                r = r + (cs[:, 0:1] + spb[:, 0:1].astype(F32))
                r_all[rows, :] = r
                return j - 1, jnp.min(r)

            def more(c2):
                j, r_min = c2
                return jnp.logical_and(j >= 0, r_min <= SB_EXIT_LOG2)

            lax.while_loop(more, step, (m - 2, jnp.min(r_all[rows, :])))
            return carry

        lax.fori_loop(2, nblk, per_block, 0)

    o_ref[0] = acc_all[...].astype(BF16)


def _sb_attention(q, k, v):
    head_spec = pl.BlockSpec((1, 1, SEQ, SB_HEAD_DIM), lambda b, h: (b, h, 0, 0))
    return pl.pallas_call(
        _sb_kernel,
        grid=(BATCH, SB_HEADS),
        in_specs=[head_spec, head_spec, head_spec],
        out_specs=pl.BlockSpec((1, SEQ, SB_HEAD_DIM), lambda b, h: (b, 0, h)),
        out_shape=jax.ShapeDtypeStruct((BATCH, SEQ, SB_DIM), BF16),
        scratch_shapes=[
            pltpu.VMEM((2 * SB_TK, SB_TK), BF16), pltpu.VMEM((2 * SB_TK, SB_TK), BF16),
            pltpu.VMEM((2 * SB_TK, SB_TK), F32), pltpu.VMEM((2 * SB_TK, SB_TK), F32),
            pltpu.VMEM((2 * SB_TK, SB_TK), F32), pltpu.VMEM((2 * SB_TK, SB_TK), F32),
            pltpu.VMEM((SB_TK, 1), F32), pltpu.VMEM((SB_TK, SB_HEAD_DIM), F32),
            pltpu.VMEM((SEQ, SB_HEAD_DIM), F32), pltpu.VMEM((SEQ, 1), F32),
        ],
        compiler_params=pltpu.CompilerParams(dimension_semantics=("parallel", "parallel")),
        name="sb_attn",
    )(q, k, v)


def _post_kernel(x_ref, mix_ref, mod_ref, wo_ref, ng_ref, wg_ref, wu_ref, wd_ref, *rest):
    n_cast = (len(rest) - 1) // 2
    o_ref = rest[n_cast]
    b = pl.program_id(0)
    x = x_ref[0]
    y = jnp.dot(mix_ref[0], wo_ref[0], preferred_element_type=F32)
    x1 = x + _mod_row(mod_ref, 2, b) * y
    h = _norm_mod(x1, ng_ref[0], _mod_row(mod_ref, 4, b), _mod_row(mod_ref, 3, b)).astype(BF16)
    gate = jnp.dot(h, wg_ref[0], preferred_element_type=F32)
    up = jnp.dot(h, wu_ref[0], preferred_element_type=F32)
    act = ((gate * _sigmoid(gate)) * up).astype(BF16)
    ffn = jnp.dot(act, wd_ref[0], preferred_element_type=F32)
    o_ref[0] = x1 + _mod_row(mod_ref, 5, b) * ffn
    _cast_blocks(rest[:n_cast], rest[n_cast + 1:])


def _post(x, mix, mod, l, w_out, norm_g, w_gate, w_up, w_down, cast_next=()):
    tm = POST_TM
    n_s = SEQ // tm
    layer = lambda b, s: (l, 0, 0)
    first = lambda b, s: (0, 0, 0)
    single = pl.Buffered(1)
    cast_in, cast_out, cast_shapes = _cast_specs(cast_next, lambda b, s: b * n_s + s, BATCH * n_s)
    outs = pl.pallas_call(
        _post_kernel,
        grid=(BATCH, n_s),
        in_specs=[
            pl.BlockSpec((1, tm, D_MODEL), lambda b, s: (b, s, 0)),
            pl.BlockSpec((1, tm, D_MODEL), lambda b, s: (b, s, 0)),
            pl.BlockSpec((1, 6, V7X_SUBLANES, D_MODEL), lambda b, s: (l, 0, 0, 0)),
            pl.BlockSpec((1, D_MODEL, D_MODEL), first, pipeline_mode=single),
            pl.BlockSpec((1, 1, D_MODEL), layer),
            pl.BlockSpec((1, D_MODEL, D_FF), first, pipeline_mode=single),
            pl.BlockSpec((1, D_MODEL, D_FF), first, pipeline_mode=single),
            pl.BlockSpec((1, D_FF, D_MODEL), first, pipeline_mode=single),
        ] + cast_in,
        out_specs=[pl.BlockSpec((1, tm, D_MODEL), lambda b, s: (b, s, 0))] + cast_out,
        out_shape=[jax.ShapeDtypeStruct((BATCH, SEQ, D_MODEL), F32)] + cast_shapes,
        compiler_params=pltpu.CompilerParams(
            dimension_semantics=("arbitrary", "arbitrary"),
            vmem_limit_bytes=58 * 1024 * 1024),
        name="post",
    )(x, mix, mod, w_out, norm_g, w_gate, w_up, w_down, *[w for w, _ in cast_next])
    return outs[0], outs[1:]


def kernel(x, c, ada_w, ada_b, norm_mix_g, norm_ffn_g, ev_w_in, ev_conv_w, ev_ret_norm_g, ev_w_out,
           od_w_qkv, od_q_norm_g, od_k_norm_g, od_w_out, ffn_w_gate, ffn_w_up, ffn_w_down):
    row3 = lambda g: g.reshape(g.shape[0], 1, g.shape[1])
    norm_mix_g, norm_ffn_g = row3(norm_mix_g), row3(norm_ffn_g)
    ev_ret_norm_g, od_q_norm_g, od_k_norm_g = row3(ev_ret_norm_g), row3(od_q_norm_g), row3(od_k_norm_g)
    ev_conv_w = jnp.pad(ev_conv_w, ((0, 0), (0, V7X_SUBLANES - CONV_WIDTH), (0, 0)))
    tables = _retention_tables()

    def f32_weights(l):
        j = l // 2
        mixer = (ev_w_in, ev_w_out) if l % 2 == 0 else (od_w_qkv, od_w_out)
        return [(mixer[0], j), (mixer[1], j), (ffn_w_gate, l), (ffn_w_up, l), (ffn_w_down, l)]

    mod, weights = _adaln(c, ada_w, ada_b, f32_weights(0))
    for l in range(DEPTH):
        j = l // 2
        w_in, w_out, w_gate, w_up, w_down = weights
        if l % 2 == 0:
            mix = _even_mix(x, mod, l, j, norm_mix_g, w_in, ev_conv_w, ev_ret_norm_g, tables)
        else:
            q, k, v = _qkv(x, mod, l, j, norm_mix_g, w_in, od_q_norm_g, od_k_norm_g)
            mix = _sb_attention(q, k, v)
        cast_next = f32_weights(l + 1) if l + 1 < DEPTH else ()
        x, weights = _post(x, mix, mod, l, w_out, norm_ffn_g, w_gate, w_up, w_down, cast_next)
    return x
```

```python
import functools
import math

import jax
import jax.numpy as jnp
import numpy as np
from jax import lax
from jax.experimental import pallas as pl
from jax.experimental.pallas import tpu as pltpu

D_MODEL = 1024
BATCH = 4
SEQ = 4096
DEPTH = 4
CHUNK = 64
EPS = 1e-6
CONV_WIDTH = 3
CONV_DIM = 512
RET_HEADS = 4
RET_HEAD_DIM = 128
RET_DIM = 512
ROPE_THETA = 10000.0
SB_HEADS = 8
SB_HEAD_DIM = 128
SB_DIM = 1024
D_FF = 2816
EVEN_IN = 3 * CONV_DIM + 4 * RET_DIM

V7X_SUBLANES = 8
V7X_BF16_SUBLANES = 16
V7X_LANES = 128
V7X_MXU_DIM = 256
V7X_VMEM_BYTES = 64 * 1024 * 1024

LOG2E = 1.0 / math.log(2.0)

EVEN_TM = 1024
RET_TILE = V7X_MXU_DIM
QKV_TM = 1024
POST_TM = 512
SB_TK = V7X_MXU_DIM
MASKED_LOGIT = -1e30
SOFTPLUS_LINEAR = 64.0
F32_MIN_SUBNORMAL_LOG2 = -149.0
SB_EXIT_LOG2 = 160.0
assert SB_EXIT_LOG2 > -F32_MIN_SUBNORMAL_LOG2 + 1

F32 = jnp.float32
BF16 = jnp.bfloat16


def _sigmoid(x):
    return 1.0 / (1.0 + jnp.exp(-x))


def _norm_mod(x, gain, scale, shift):
    ms = jnp.mean(x * x, axis=-1, keepdims=True)
    return (x * lax.rsqrt(ms + EPS)) * (gain * (1.0 + scale)) + shift


def _mod_row(mod_ref, k, b):
    return mod_ref[0, k, pl.ds(b, 1), :]


def _cast_specs(cast, step_of, n_steps):
    in_specs, out_specs, shapes = [], [], []
    for w, li in cast:
        _, rows, cols = w.shape
        n = 1
        while 2 * n <= n_steps and rows % (2 * n * V7X_BF16_SUBLANES) == 0:
            n *= 2
        chunk = lambda *ids, n=n: jnp.minimum(step_of(*ids), n - 1)
        in_specs.append(pl.BlockSpec((1, rows // n, cols),
                                     lambda *ids, li=li, chunk=chunk: (li, chunk(*ids), 0)))
        out_specs.append(pl.BlockSpec((1, rows // n, cols),
                                      lambda *ids, chunk=chunk: (0, chunk(*ids), 0)))
        shapes.append(jax.ShapeDtypeStruct((1, rows, cols), BF16))
    return in_specs, out_specs, shapes


def _cast_blocks(src_refs, dst_refs):
    for src_ref, dst_ref in zip(src_refs, dst_refs):
        dst_ref[...] = src_ref[...].astype(BF16)


def _adaln_kernel(ct_ref, w_ref, b_ref, *rest):
    n_cast = (len(rest) - 1) // 2
    o_ref = rest[n_cast]
    ct = ct_ref[...]
    cat = ct * _sigmoid(ct)
    w = w_ref[0]
    rows = [jnp.sum(w * cat[:, b:b + 1], axis=0, keepdims=True) for b in range(BATCH)]
    rows += [jnp.zeros((V7X_SUBLANES - BATCH, D_MODEL), F32)]
    o_ref[0, 0] = jnp.concatenate(rows, axis=0) + b_ref[0, 0]
    _cast_blocks(rest[:n_cast], rest[n_cast + 1:])


def _adaln(c, ada_w, ada_b, cast_first=()):
    c_pad = jnp.pad(c, ((0, V7X_SUBLANES - BATCH), (0, 0))).T
    b4 = ada_b.reshape(DEPTH, 6, 1, D_MODEL)
    cast_in, cast_out, cast_shapes = _cast_specs(cast_first, lambda l, k: l * 6 + k, DEPTH * 6)
    outs = pl.pallas_call(
        _adaln_kernel,
        grid=(DEPTH, 6),
        in_specs=[
            pl.BlockSpec((D_MODEL, V7X_SUBLANES), lambda l, k: (0, 0)),
            pl.BlockSpec((1, D_MODEL, D_MODEL), lambda l, k: (l, 0, k)),
            pl.BlockSpec((1, 1, 1, D_MODEL), lambda l, k: (l, k, 0, 0)),
        ] + cast_in,
        out_specs=[pl.BlockSpec((1, 1, V7X_SUBLANES, D_MODEL), lambda l, k: (l, k, 0, 0))] + cast_out,
        out_shape=[jax.ShapeDtypeStruct((DEPTH, 6, V7X_SUBLANES, D_MODEL), F32)] + cast_shapes,
        compiler_params=pltpu.CompilerParams(dimension_semantics=("arbitrary", "arbitrary")),
        name="adaln",
    )(c_pad, ada_w, b4, *[w for w, _ in cast_first])
    return outs[0], outs[1:]


def _retention_tables():
    h = np.arange(RET_HEADS, dtype=np.float64)
    log_g = np.log1p(-np.exp2(-5.0 - h))
    idx = np.arange(RET_TILE, dtype=np.float64)
    diff = idx[:, None] - idx[None, :]
    same_or_earlier = (idx[None, :] // CHUNK) <= (idx[:, None] // CHUNK)
    k_scale = RET_HEAD_DIM ** -0.5
    dec = np.exp(np.abs(diff)[None] * log_g[:, None, None]) * same_or_earlier[None] * k_scale
    qdec = np.exp((idx + 1.0)[None, :] * log_g[:, None])
    kdec = np.exp((RET_TILE - 1.0 - idx)[None, :] * log_g[:, None]) * k_scale
    sdec = np.exp(RET_TILE * log_g)
    qdec = np.broadcast_to(qdec[:, :, None], (RET_HEADS, RET_TILE, V7X_LANES))
    kdec = np.broadcast_to(kdec[:, :, None], (RET_HEADS, RET_TILE, V7X_LANES))
    sdec = np.broadcast_to(sdec[:, None, None], (RET_HEADS, V7X_SUBLANES, V7X_LANES))
    inv_freq = 1.0 / (ROPE_THETA ** (np.arange(0, RET_HEAD_DIM, 2, dtype=np.float64) / RET_HEAD_DIM))
    ang = np.arange(SEQ, dtype=np.float64)[:, None] * inv_freq[None, :]
    cos2 = np.concatenate([np.cos(ang), np.cos(ang)], axis=-1)
    sin2 = np.concatenate([-np.sin(ang), np.sin(ang)], axis=-1)
    rot = np.stack([cos2, sin2])
    f = lambda a: jnp.asarray(np.ascontiguousarray(a), dtype=F32)
    return f(dec), f(qdec), f(kdec), f(sdec), f(rot)


def _even_kernel(x_ref, mod_ref, ng_ref, win_ref, cw_ref, rg_ref, rot_ref, dec_ref, qdec_ref,
                 kdec_ref, sdec_ref, *rest):
    n_cast = (len(rest) - 3) // 2
    mix_ref = rest[n_cast]
    state_ref, zbuf_ref = rest[-2:]
    _cast_blocks(rest[:n_cast], rest[n_cast + 1:2 * n_cast + 1])
    b = pl.program_id(0)
    si = pl.program_id(1)
    tm = EVEN_TM

    @pl.when(si == 0)
    def _():
        state_ref[...] = jnp.zeros_like(state_ref)
        zbuf_ref[0:V7X_SUBLANES, :] = jnp.zeros((V7X_SUBLANES, CONV_DIM), F32)

    q_off = 3 * CONV_DIM
    k_off = q_off + RET_DIM
    v_off = k_off + RET_DIM
    g_off = v_off + RET_DIM
    cw = cw_ref[0]
    sub_tiles = [slice(r0, r0 + RET_TILE) for r0 in range(0, tm, RET_TILE)]

    projs = []
    for rows in sub_tiles:
        h = _norm_mod(x_ref[0, rows, :], ng_ref[0], _mod_row(mod_ref, 1, b),
                      _mod_row(mod_ref, 0, b)).astype(BF16)
        projs.append(jnp.dot(h, win_ref[0], preferred_element_type=F32))

    for rows, proj in zip(sub_tiles, projs):
        r0 = rows.start
        zrows = lambda back: slice(V7X_SUBLANES + r0 - back, V7X_SUBLANES + r0 - back + RET_TILE)
        z = proj[:, CONV_DIM:2 * CONV_DIM] * proj[:, 2 * CONV_DIM:3 * CONV_DIM]
        zbuf_ref[zrows(0), :] = z
        y = cw[0:1, :] * zbuf_ref[zrows(2), :] + cw[1:2, :] * zbuf_ref[zrows(1), :] + cw[2:3, :] * z
        mix_ref[0, rows, 0:CONV_DIM] = (proj[:, 0:CONV_DIM] * y).astype(BF16)

        cos = rot_ref[0, rows, :]
        sin = rot_ref[1, rows, :]
        for hh in range(RET_HEADS):
            c0 = hh * RET_HEAD_DIM
            q = proj[:, q_off + c0:q_off + c0 + RET_HEAD_DIM]
            k = proj[:, k_off + c0:k_off + c0 + RET_HEAD_DIM]
            v = proj[:, v_off + c0:v_off + c0 + RET_HEAD_DIM].astype(BF16)
            g = proj[:, g_off + c0:g_off + c0 + RET_HEAD_DIM]
            qr = q * cos + pltpu.roll(q, RET_HEAD_DIM // 2, axis=1) * sin
            kr = k * cos + pltpu.roll(k, RET_HEAD_DIM // 2, axis=1) * sin
            s = lax.dot_general(qr.astype(BF16), kr.astype(BF16), (((1,), (1,)), ((), ())),
                                preferred_element_type=F32)
            s = s * dec_ref[hh]
            o = jnp.dot(s.astype(BF16), v, preferred_element_type=F32)
            st = state_ref[hh]
            o = o + jnp.dot((qr * qdec_ref[hh]).astype(BF16), st.astype(BF16),
                            preferred_element_type=F32)
            kd = (kr * kdec_ref[hh]).astype(BF16)
            kv = lax.dot_general(kd, v, (((0,), (0,)), ((), ())), preferred_element_type=F32)
            state_ref[hh] = st * sdec_ref[hh, 0:1, :] + kv
            ms = jnp.mean(o * o, axis=-1, keepdims=True)
            rn = (o * lax.rsqrt(ms + EPS)) * rg_ref[0, :, c0:c0 + RET_HEAD_DIM]
            out = (g * _sigmoid(g)) * rn
            mix_ref[0, rows, CONV_DIM + c0:CONV_DIM + c0 + RET_HEAD_DIM] = out.astype(BF16)
    zbuf_ref[0:V7X_SUBLANES, :] = zbuf_ref[tm:tm + V7X_SUBLANES, :]


def _even_mix(x, mod, l, j, norm_g, w_in, conv_w, ret_g, tables, cast=()):
    dec, qdec, kdec, sdec, rot = tables
    tm = EVEN_TM
    n_s = SEQ // tm
    const3 = lambda b, s: (0, 0, 0)
    layer = lambda b, s: (l, 0, 0)
    even = lambda b, s: (j, 0, 0)
    cast_in, cast_out, cast_shapes = _cast_specs(cast, lambda b, s: b * n_s + s, BATCH * n_s)
    outs = pl.pallas_call(
        _even_kernel,
        grid=(BATCH, SEQ // tm),
        in_specs=[
            pl.BlockSpec((1, tm, D_MODEL), lambda b, s: (b, s, 0)),
            pl.BlockSpec((1, 6, V7X_SUBLANES, D_MODEL), lambda b, s: (l, 0, 0, 0)),
            pl.BlockSpec((1, 1, D_MODEL), layer),
            pl.BlockSpec((1, D_MODEL, EVEN_IN), const3, pipeline_mode=pl.Buffered(1)),
            pl.BlockSpec((1, V7X_SUBLANES, CONV_DIM), even),
            pl.BlockSpec((1, 1, RET_DIM), even),
            pl.BlockSpec((2, tm, RET_HEAD_DIM), lambda b, s: (0, s, 0)),
            pl.BlockSpec((RET_HEADS, RET_TILE, RET_TILE), const3),
            pl.BlockSpec((RET_HEADS, RET_TILE, V7X_LANES), const3),
            pl.BlockSpec((RET_HEADS, RET_TILE, V7X_LANES), const3),
            pl.BlockSpec((RET_HEADS, V7X_SUBLANES, V7X_LANES), const3),
        ] + cast_in,
        out_specs=[pl.BlockSpec((1, tm, D_MODEL), lambda b, s: (b, s, 0))] + cast_out,
        out_shape=[jax.ShapeDtypeStruct((BATCH, SEQ, D_MODEL), BF16)] + cast_shapes,
        scratch_shapes=[
            pltpu.VMEM((RET_HEADS, RET_HEAD_DIM, RET_HEAD_DIM), F32),
            pltpu.VMEM((tm + V7X_SUBLANES, CONV_DIM), F32),
        ],
        compiler_params=pltpu.CompilerParams(
            dimension_semantics=("arbitrary", "arbitrary"),
            vmem_limit_bytes=56 * 1024 * 1024),
        name="even_mix",
    )(x, mod, norm_g, w_in, conv_w, ret_g, rot, dec, qdec, kdec, sdec, *[w for w, _ in cast])
    return outs[0], outs[1:]


def _qkv_kernel(x_ref, mod_ref, ng_ref, w_ref, qg_ref, kg_ref, q_ref, k_ref, v_ref):
    b = pl.program_id(0)
    x = x_ref[0]
    h = _norm_mod(x, ng_ref[0], _mod_row(mod_ref, 1, b), _mod_row(mod_ref, 0, b)).astype(BF16)
    qkv = jnp.dot(h, w_ref[0], preferred_element_type=F32)
    qg = qg_ref[0] * (SB_HEAD_DIM ** -0.5 * LOG2E)
    kg = kg_ref[0]
    for hh in range(SB_HEADS):
        c0 = hh * SB_HEAD_DIM
        q = qkv[:, c0:c0 + SB_HEAD_DIM]
        k = qkv[:, SB_DIM + c0:SB_DIM + c0 + SB_HEAD_DIM]
        v = qkv[:, 2 * SB_DIM + c0:2 * SB_DIM + c0 + SB_HEAD_DIM]
        qn = (q * lax.rsqrt(jnp.mean(q * q, axis=-1, keepdims=True) + EPS)) * qg
        kn = (k * lax.rsqrt(jnp.mean(k * k, axis=-1, keepdims=True) + EPS)) * kg
        q_ref[0, hh] = qn.astype(BF16)
        k_ref[0, hh] = kn.astype(BF16)
        v_ref[0, hh] = v.astype(BF16)


def _qkv(x, mod, l, j, norm_g, w_qkv, q_g, k_g):
    tm = QKV_TM
    layer = lambda b, s: (l, 0, 0)
    odd = lambda b, s: (j, 0, 0)
    head_spec = pl.BlockSpec((1, SB_HEADS, tm, SB_HEAD_DIM), lambda b, s: (b, 0, s, 0))
    head_shape = jax.ShapeDtypeStruct((BATCH, SB_HEADS, SEQ, SB_HEAD_DIM), BF16)
    return pl.pallas_call(
        _qkv_kernel,
        grid=(BATCH, SEQ // tm),
        in_specs=[
            pl.BlockSpec((1, tm, D_MODEL), lambda b, s: (b, s, 0)),
            pl.BlockSpec((1, 6, V7X_SUBLANES, D_MODEL), lambda b, s: (l, 0, 0, 0)),
            pl.BlockSpec((1, 1, D_MODEL), layer),
            pl.BlockSpec((1, D_MODEL, 3 * SB_DIM), lambda b, s: (0, 0, 0),
                         pipeline_mode=pl.Buffered(1)),
            pl.BlockSpec((1, 1, SB_HEAD_DIM), odd),
            pl.BlockSpec((1, 1, SB_HEAD_DIM), odd),
        ],
        out_specs=[head_spec, head_spec, head_spec],
        out_shape=[head_shape, head_shape, head_shape],
        compiler_params=pltpu.CompilerParams(
            dimension_semantics=("parallel", "parallel"),
            vmem_limit_bytes=56 * 1024 * 1024),
        name="qkv",
    )(x, mod, norm_g, w_qkv, q_g, k_g)


def _softplus2(z):
    return jnp.maximum(z, jnp.log(1.0 + jnp.exp2(jnp.minimum(z, SOFTPLUS_LINEAR))) * LOG2E)


def _sb_kernel(q_ref, k_ref, v_ref, o_ref, spb0, spb1, lb0, lb1, a0, a1, rc_ref, ac_ref,
               acc_all, r_all):
    row = lax.broadcasted_iota(jnp.int32, (SB_TK, SB_TK), 0)
    col = lax.broadcasted_iota(jnp.int32, (SB_TK, SB_TK), 1)
    later = (row > col).astype(BF16)
    causal = col < row
    sets = ((spb0, lb0), (spb1, lb1))
    abuf = (a0, a1)
    nblk = SEQ // SB_TK

    def blk(ref, j, rows=SB_TK):
        return ref[0, 0, pl.ds(pl.multiple_of(j * SB_TK, SB_TK), rows), :]

    def qk(q, j):
        return lax.dot_general(q, blk(k_ref, j), (((1,), (1,)), ((), ())),
                               preferred_element_type=F32)

    def stage_a(j, i, m):
        spb_ref, lb_ref = sets[i]
        z = qk(blk(q_ref, j, m), j)
        z_top = jnp.where(causal, z[:SB_TK], MASKED_LOGIT)
        sp_top = _softplus2(z_top)
        spb_ref[0:SB_TK] = sp_top.astype(BF16)
        lb_ref[0:SB_TK] = z_top - sp_top
        if m > SB_TK:
            sp = _softplus2(z[SB_TK:])
            spb_ref[SB_TK:m] = sp.astype(BF16)
            lb_ref[SB_TK:m] = z[SB_TK:] - sp

    def stage_b(j, i, m):
        spb_ref, lb_ref = sets[i]
        cs = jnp.dot(spb_ref[0:m], later, preferred_element_type=F32)
        rowsum = cs[:, 0:1] + spb_ref[0:m, 0:1].astype(F32)
        abuf[i][0:SB_TK] = lb_ref[0:SB_TK] - cs[:SB_TK]
        if m > SB_TK:
            r_old = rc_ref[...]
            abuf[i][SB_TK:m] = (lb_ref[SB_TK:m] - cs[SB_TK:]) - r_old
            r_all[pl.ds(pl.multiple_of((j + 1) * SB_TK, SB_TK), SB_TK), :] = r_old + rowsum[SB_TK:]
        rc_ref[...] = rowsum[:SB_TK]

    def stage_c(j, i, m):
        w = jnp.exp2(abuf[i][0:m]).astype(BF16)
        pv = jnp.dot(w, blk(v_ref, j), preferred_element_type=F32)
        if m > SB_TK:
            acc_all[pl.ds(pl.multiple_of((j + 1) * SB_TK, SB_TK), SB_TK), :] = ac_ref[...] + pv[SB_TK:]
        ac_ref[...] = pv[:SB_TK]

    def tick(tau, par, a=True, b=True, c=True):
        rows = lambda t: SB_TK if (isinstance(t, int) and t == 0) else 2 * SB_TK
        if a:
            stage_a(nblk - 1 - tau, par, rows(tau))
        if b:
            stage_b(nblk - tau, 1 - par, rows(tau - 1))
        if c:
            stage_c(nblk + 1 - tau, par, rows(tau - 2))

    for tau in range(nblk + 2):
        tick(tau, tau % 2, a=tau < nblk, b=1 <= tau <= nblk, c=tau >= 2)
    acc_all[0:SB_TK] = ac_ref[...]

    r_all[0:2 * SB_TK] = jnp.full((2 * SB_TK, 1), 2 * SB_EXIT_LOG2, F32)

    @pl.when(jnp.min(r_all[...]) <= SB_EXIT_LOG2)
    def _():
        def per_block(m, carry):
            rows = pl.ds(pl.multiple_of(m * SB_TK, SB_TK), SB_TK)

            def step(c2):
                j, _ = c2
                z = qk(blk(q_ref, m), j)
                sp = _softplus2(z)
                spb = sp.astype(BF16)
                cs = jnp.dot(spb, later, preferred_element_type=F32)
                r = r_all[rows, :]
                w = jnp.exp2(((z - sp) - cs) - r).astype(BF16)
                acc_all[rows, :] += jnp.dot(w, blk(v_ref, j), preferred_element_type=F32)
                r = r + (cs[:, 0:1] + spb[:, 0:1].astype(F32))
                r_all[rows, :] = r
                return j - 1, jnp.min(r)

            def more(c2):
                j, r_min = c2
                return jnp.logical_and(j >= 0, r_min <= SB_EXIT_LOG2)

            lax.while_loop(more, step, (m - 2, jnp.min(r_all[rows, :])))
            return carry

        lax.fori_loop(2, nblk, per_block, 0)

    o_ref[0] = acc_all[...].astype(BF16)


def _sb_attention(q, k, v):
    head_spec = pl.BlockSpec((1, 1, SEQ, SB_HEAD_DIM), lambda b, h: (b, h, 0, 0))
    return pl.pallas_call(
        _sb_kernel,
        grid=(BATCH, SB_HEADS),
        in_specs=[head_spec, head_spec, head_spec],
        out_specs=pl.BlockSpec((1, SEQ, SB_HEAD_DIM), lambda b, h: (b, 0, h)),
        out_shape=jax.ShapeDtypeStruct((BATCH, SEQ, SB_DIM), BF16),
        scratch_shapes=[
            pltpu.VMEM((2 * SB_TK, SB_TK), BF16), pltpu.VMEM((2 * SB_TK, SB_TK), BF16),
            pltpu.VMEM((2 * SB_TK, SB_TK), F32), pltpu.VMEM((2 * SB_TK, SB_TK), F32),
            pltpu.VMEM((2 * SB_TK, SB_TK), F32), pltpu.VMEM((2 * SB_TK, SB_TK), F32),
            pltpu.VMEM((SB_TK, 1), F32), pltpu.VMEM((SB_TK, SB_HEAD_DIM), F32),
            pltpu.VMEM((SEQ, SB_HEAD_DIM), F32), pltpu.VMEM((SEQ, 1), F32),
        ],
        compiler_params=pltpu.CompilerParams(dimension_semantics=("parallel", "parallel")),
        name="sb_attn",
    )(q, k, v)


def _post_kernel(x_ref, mix_ref, mod_ref, wo_ref, ng_ref, wg_ref, wu_ref, wd_ref, *rest):
    n_cast = (len(rest) - 1) // 2
    o_ref = rest[n_cast]
    b = pl.program_id(0)
    x = x_ref[0]
    y = jnp.dot(mix_ref[0], wo_ref[0], preferred_element_type=F32)
    x1 = x + _mod_row(mod_ref, 2, b) * y
    h = _norm_mod(x1, ng_ref[0], _mod_row(mod_ref, 4, b), _mod_row(mod_ref, 3, b)).astype(BF16)
    gate = jnp.dot(h, wg_ref[0], preferred_element_type=F32)
    up = jnp.dot(h, wu_ref[0], preferred_element_type=F32)
    act = ((gate * _sigmoid(gate)) * up).astype(BF16)
    ffn = jnp.dot(act, wd_ref[0], preferred_element_type=F32)
    o_ref[0] = x1 + _mod_row(mod_ref, 5, b) * ffn
    _cast_blocks(rest[:n_cast], rest[n_cast + 1:])


def _post(x, mix, mod, l, w_out, norm_g, w_gate, w_up, w_down, cast_next=()):
    tm = POST_TM
    n_s = SEQ // tm
    layer = lambda b, s: (l, 0, 0)
    first = lambda b, s: (0, 0, 0)
    single = pl.Buffered(1)
    cast_in, cast_out, cast_shapes = _cast_specs(cast_next, lambda b, s: b * n_s + s, BATCH * n_s)
    outs = pl.pallas_call(
        _post_kernel,
        grid=(BATCH, n_s),
        in_specs=[
            pl.BlockSpec((1, tm, D_MODEL), lambda b, s: (b, s, 0)),
            pl.BlockSpec((1, tm, D_MODEL), lambda b, s: (b, s, 0)),
            pl.BlockSpec((1, 6, V7X_SUBLANES, D_MODEL), lambda b, s: (l, 0, 0, 0)),
            pl.BlockSpec((1, D_MODEL, D_MODEL), first, pipeline_mode=single),
            pl.BlockSpec((1, 1, D_MODEL), layer),
            pl.BlockSpec((1, D_MODEL, D_FF), first, pipeline_mode=single),
            pl.BlockSpec((1, D_MODEL, D_FF), first, pipeline_mode=single),
            pl.BlockSpec((1, D_FF, D_MODEL), first, pipeline_mode=single),
        ] + cast_in,
        out_specs=[pl.BlockSpec((1, tm, D_MODEL), lambda b, s: (b, s, 0))] + cast_out,
        out_shape=[jax.ShapeDtypeStruct((BATCH, SEQ, D_MODEL), F32)] + cast_shapes,
        compiler_params=pltpu.CompilerParams(
            dimension_semantics=("arbitrary", "arbitrary"),
            vmem_limit_bytes=58 * 1024 * 1024),
        name="post",
    )(x, mix, mod, w_out, norm_g, w_gate, w_up, w_down, *[w for w, _ in cast_next])
    return outs[0], outs[1:]


def kernel(x, c, ada_w, ada_b, norm_mix_g, norm_ffn_g, ev_w_in, ev_conv_w, ev_ret_norm_g, ev_w_out,
           od_w_qkv, od_q_norm_g, od_k_norm_g, od_w_out, ffn_w_gate, ffn_w_up, ffn_w_down):
    row3 = lambda g: g.reshape(g.shape[0], 1, g.shape[1])
    norm_mix_g, norm_ffn_g = row3(norm_mix_g), row3(norm_ffn_g)
    ev_ret_norm_g, od_q_norm_g, od_k_norm_g = row3(ev_ret_norm_g), row3(od_q_norm_g), row3(od_k_norm_g)
    ev_conv_w = jnp.pad(ev_conv_w, ((0, 0), (0, V7X_SUBLANES - CONV_WIDTH), (0, 0)))
    tables = _retention_tables()

    def f32_weights(l):
        j = l // 2
        mixer = (ev_w_in, ev_w_out) if l % 2 == 0 else (od_w_qkv, od_w_out)
        return [(mixer[0], j), (mixer[1], j), (ffn_w_gate, l), (ffn_w_up, l), (ffn_w_down, l)]

    mod, weights = _adaln(c, ada_w, ada_b, f32_weights(0)[:2])
    for l in range(DEPTH):
        j = l // 2
        w_in, w_out, *ffn = weights
        if l % 2 == 0:
            mix, ffn0 = _even_mix(x, mod, l, j, norm_mix_g, w_in, ev_conv_w, ev_ret_norm_g, tables,
                                  cast=f32_weights(0)[2:] if l == 0 else ())
            ffn = ffn or ffn0
        else:
            q, k, v = _qkv(x, mod, l, j, norm_mix_g, w_in, od_q_norm_g, od_k_norm_g)
            mix = _sb_attention(q, k, v)
        w_gate, w_up, w_down = ffn
        cast_next = f32_weights(l + 1) if l + 1 < DEPTH else ()
        x, weights = _post(x, mix, mod, l, w_out, norm_ffn_g, w_gate, w_up, w_down, cast_next)
    return x
```

```python
import functools
import math

import jax
import jax.numpy as jnp
import numpy as np
from jax import lax
from jax.experimental import pallas as pl
from jax.experimental.pallas import tpu as pltpu

D_MODEL = 1024
BATCH = 4
SEQ = 4096
DEPTH = 4
CHUNK = 64
EPS = 1e-6
CONV_WIDTH = 3
CONV_DIM = 512
RET_HEADS = 4
RET_HEAD_DIM = 128
RET_DIM = 512
ROPE_THETA = 10000.0
SB_HEADS = 8
SB_HEAD_DIM = 128
SB_DIM = 1024
D_FF = 2816
EVEN_IN = 3 * CONV_DIM + 4 * RET_DIM

V7X_SUBLANES = 8
V7X_BF16_SUBLANES = 16
V7X_LANES = 128
V7X_MXU_DIM = 256
V7X_VMEM_BYTES = 64 * 1024 * 1024

LOG2E = 1.0 / math.log(2.0)

EVEN_TM = 1024
RET_TILE = V7X_MXU_DIM
QKV_TM = 1024
POST_TM = 512
SB_TK = V7X_MXU_DIM
MASKED_LOGIT = -1e30
SOFTPLUS_LINEAR = 64.0
F32_MIN_SUBNORMAL_LOG2 = -149.0
SB_EXIT_LOG2 = 160.0
assert SB_EXIT_LOG2 > -F32_MIN_SUBNORMAL_LOG2 + 1

F32 = jnp.float32
BF16 = jnp.bfloat16


def _sigmoid(x):
    return 1.0 / (1.0 + jnp.exp(-x))


def _norm_mod(x, gain, scale, shift):
    ms = jnp.mean(x * x, axis=-1, keepdims=True)
    return (x * lax.rsqrt(ms + EPS)) * (gain * (1.0 + scale)) + shift


def _mod_row(mod_ref, k, b):
    return mod_ref[0, k, pl.ds(b, 1), :]


def _cast_specs(cast, step_of, n_steps):
    in_specs, out_specs, shapes = [], [], []
    for w, li in cast:
        _, rows, cols = w.shape
        n = 1
        while 2 * n <= n_steps and rows % (2 * n * V7X_BF16_SUBLANES) == 0:
            n *= 2
        chunk = lambda *ids, n=n: jnp.minimum(step_of(*ids), n - 1)
        in_specs.append(pl.BlockSpec((1, rows // n, cols),
                                     lambda *ids, li=li, chunk=chunk: (li, chunk(*ids), 0)))
        out_specs.append(pl.BlockSpec((1, rows // n, cols),
                                      lambda *ids, chunk=chunk: (0, chunk(*ids), 0)))
        shapes.append(jax.ShapeDtypeStruct((1, rows, cols), BF16))
    return in_specs, out_specs, shapes


def _cast_blocks(src_refs, dst_refs):
    for src_ref, dst_ref in zip(src_refs, dst_refs):
        dst_ref[...] = src_ref[...].astype(BF16)


def _adaln_kernel(ct_ref, w_ref, b_ref, *rest):
    n_cast = (len(rest) - 1) // 2
    o_ref = rest[n_cast]
    ct = ct_ref[...]
    cat = ct * _sigmoid(ct)
    w = w_ref[0]
    rows = [jnp.sum(w * cat[:, b:b + 1], axis=0, keepdims=True) for b in range(BATCH)]
    rows += [jnp.zeros((V7X_SUBLANES - BATCH, D_MODEL), F32)]
    o_ref[0, 0] = jnp.concatenate(rows, axis=0) + b_ref[0, 0]
    _cast_blocks(rest[:n_cast], rest[n_cast + 1:])


def _adaln(c, ada_w, ada_b, cast_first=()):
    c_pad = jnp.pad(c, ((0, V7X_SUBLANES - BATCH), (0, 0))).T
    b4 = ada_b.reshape(DEPTH, 6, 1, D_MODEL)
    cast_in, cast_out, cast_shapes = _cast_specs(cast_first, lambda l, k: l * 6 + k, DEPTH * 6)
    outs = pl.pallas_call(
        _adaln_kernel,
        grid=(DEPTH, 6),
        in_specs=[
            pl.BlockSpec((D_MODEL, V7X_SUBLANES), lambda l, k: (0, 0)),
            pl.BlockSpec((1, D_MODEL, D_MODEL), lambda l, k: (l, 0, k)),
            pl.BlockSpec((1, 1, 1, D_MODEL), lambda l, k: (l, k, 0, 0)),
        ] + cast_in,
        out_specs=[pl.BlockSpec((1, 1, V7X_SUBLANES, D_MODEL), lambda l, k: (l, k, 0, 0))] + cast_out,
        out_shape=[jax.ShapeDtypeStruct((DEPTH, 6, V7X_SUBLANES, D_MODEL), F32)] + cast_shapes,
        compiler_params=pltpu.CompilerParams(dimension_semantics=("arbitrary", "arbitrary")),
        name="adaln",
    )(c_pad, ada_w, b4, *[w for w, _ in cast_first])
    return outs[0], outs[1:]


def _retention_tables():
    h = np.arange(RET_HEADS, dtype=np.float64)
    log_g = np.log1p(-np.exp2(-5.0 - h))
    idx = np.arange(RET_TILE, dtype=np.float64)
    diff = idx[:, None] - idx[None, :]
    same_or_earlier = (idx[None, :] // CHUNK) <= (idx[:, None] // CHUNK)
    k_scale = RET_HEAD_DIM ** -0.5
    dec = np.exp(np.abs(diff)[None] * log_g[:, None, None]) * same_or_earlier[None] * k_scale
    qdec = np.exp((idx + 1.0)[None, :] * log_g[:, None])
    kdec = np.exp((RET_TILE - 1.0 - idx)[None, :] * log_g[:, None]) * k_scale
    sdec = np.exp(RET_TILE * log_g)
    qdec = np.broadcast_to(qdec[:, :, None], (RET_HEADS, RET_TILE, V7X_LANES))
    kdec = np.broadcast_to(kdec[:, :, None], (RET_HEADS, RET_TILE, V7X_LANES))
    sdec = np.broadcast_to(sdec[:, None, None], (RET_HEADS, V7X_SUBLANES, V7X_LANES))
    inv_freq = 1.0 / (ROPE_THETA ** (np.arange(0, RET_HEAD_DIM, 2, dtype=np.float64) / RET_HEAD_DIM))
    ang = np.arange(SEQ, dtype=np.float64)[:, None] * inv_freq[None, :]
    cos2 = np.concatenate([np.cos(ang), np.cos(ang)], axis=-1)
    sin2 = np.concatenate([-np.sin(ang), np.sin(ang)], axis=-1)
    rot = np.stack([cos2, sin2])
    f = lambda a: jnp.asarray(np.ascontiguousarray(a), dtype=F32)
    return f(dec), f(qdec), f(kdec), f(sdec), f(rot)


def _even_kernel(x_ref, mod_ref, ng_ref, win_ref, cw_ref, rg_ref, rot_ref, dec_ref, qdec_ref,
                 kdec_ref, sdec_ref, *rest):
    n_cast = (len(rest) - 3) // 2
    mix_ref = rest[n_cast]
    state_ref, zbuf_ref = rest[-2:]
    _cast_blocks(rest[:n_cast], rest[n_cast + 1:2 * n_cast + 1])
    b = pl.program_id(0)
    si = pl.program_id(1)
    tm = EVEN_TM

    @pl.when(si == 0)
    def _():
        state_ref[...] = jnp.zeros_like(state_ref)
        zbuf_ref[0:V7X_SUBLANES, :] = jnp.zeros((V7X_SUBLANES, CONV_DIM), F32)

    q_off = 3 * CONV_DIM
    k_off = q_off + RET_DIM
    v_off = k_off + RET_DIM
    g_off = v_off + RET_DIM
    cw = cw_ref[0]
    sub_tiles = [slice(r0, r0 + RET_TILE) for r0 in range(0, tm, RET_TILE)]

    projs = []
    for rows in sub_tiles:
        h = _norm_mod(x_ref[0, rows, :], ng_ref[0], _mod_row(mod_ref, 1, b),
                      _mod_row(mod_ref, 0, b)).astype(BF16)
        projs.append(jnp.dot(h, win_ref[0], preferred_element_type=F32))

    for rows, proj in zip(sub_tiles, projs):
        r0 = rows.start
        zrows = lambda back: slice(V7X_SUBLANES + r0 - back, V7X_SUBLANES + r0 - back + RET_TILE)
        z = proj[:, CONV_DIM:2 * CONV_DIM] * proj[:, 2 * CONV_DIM:3 * CONV_DIM]
        zbuf_ref[zrows(0), :] = z
        y = cw[0:1, :] * zbuf_ref[zrows(2), :] + cw[1:2, :] * zbuf_ref[zrows(1), :] + cw[2:3, :] * z
        mix_ref[0, rows, 0:CONV_DIM] = (proj[:, 0:CONV_DIM] * y).astype(BF16)

        cos = rot_ref[0, rows, :]
        sin = rot_ref[1, rows, :]
        for hh in range(RET_HEADS):
            c0 = hh * RET_HEAD_DIM
            q = proj[:, q_off + c0:q_off + c0 + RET_HEAD_DIM]
            k = proj[:, k_off + c0:k_off + c0 + RET_HEAD_DIM]
            v = proj[:, v_off + c0:v_off + c0 + RET_HEAD_DIM].astype(BF16)
            g = proj[:, g_off + c0:g_off + c0 + RET_HEAD_DIM]
            qr = q * cos + pltpu.roll(q, RET_HEAD_DIM // 2, axis=1) * sin
            kr = k * cos + pltpu.roll(k, RET_HEAD_DIM // 2, axis=1) * sin
            s = lax.dot_general(qr.astype(BF16), kr.astype(BF16), (((1,), (1,)), ((), ())),
                                preferred_element_type=F32)
            s = s * dec_ref[hh]
            o = jnp.dot(s.astype(BF16), v, preferred_element_type=F32)
            st = state_ref[hh]
            o = o + jnp.dot((qr * qdec_ref[hh]).astype(BF16), st.astype(BF16),
                            preferred_element_type=F32)
            kd = (kr * kdec_ref[hh]).astype(BF16)
            kv = lax.dot_general(kd, v, (((0,), (0,)), ((), ())), preferred_element_type=F32)
            state_ref[hh] = st * sdec_ref[hh, 0:1, :] + kv
            ms = jnp.mean(o * o, axis=-1, keepdims=True)
            rn = (o * lax.rsqrt(ms + EPS)) * rg_ref[0, :, c0:c0 + RET_HEAD_DIM]
            out = (g * _sigmoid(g)) * rn
            mix_ref[0, rows, CONV_DIM + c0:CONV_DIM + c0 + RET_HEAD_DIM] = out.astype(BF16)
    zbuf_ref[0:V7X_SUBLANES, :] = zbuf_ref[tm:tm + V7X_SUBLANES, :]


def _even_mix(x, mod, l, j, norm_g, w_in, conv_w, ret_g, tables, cast=()):
    dec, qdec, kdec, sdec, rot = tables
    tm = EVEN_TM
    n_s = SEQ // tm
    const3 = lambda b, s: (0, 0, 0)
    layer = lambda b, s: (l, 0, 0)
    even = lambda b, s: (j, 0, 0)
    cast_in, cast_out, cast_shapes = _cast_specs(cast, lambda b, s: b * n_s + s, BATCH * n_s)
    outs = pl.pallas_call(
        _even_kernel,
        grid=(BATCH, SEQ // tm),
        in_specs=[
            pl.BlockSpec((1, tm, D_MODEL), lambda b, s: (b, s, 0)),
            pl.BlockSpec((1, 6, V7X_SUBLANES, D_MODEL), lambda b, s: (l, 0, 0, 0)),
            pl.BlockSpec((1, 1, D_MODEL), layer),
            pl.BlockSpec((1, D_MODEL, EVEN_IN), const3, pipeline_mode=pl.Buffered(1)),
            pl.BlockSpec((1, V7X_SUBLANES, CONV_DIM), even),
            pl.BlockSpec((1, 1, RET_DIM), even),
            pl.BlockSpec((2, tm, RET_HEAD_DIM), lambda b, s: (0, s, 0)),
            pl.BlockSpec((RET_HEADS, RET_TILE, RET_TILE), const3),
            pl.BlockSpec((RET_HEADS, RET_TILE, V7X_LANES), const3),
            pl.BlockSpec((RET_HEADS, RET_TILE, V7X_LANES), const3),
            pl.BlockSpec((RET_HEADS, V7X_SUBLANES, V7X_LANES), const3),
        ] + cast_in,
        out_specs=[pl.BlockSpec((1, tm, D_MODEL), lambda b, s: (b, s, 0))] + cast_out,
        out_shape=[jax.ShapeDtypeStruct((BATCH, SEQ, D_MODEL), BF16)] + cast_shapes,
        scratch_shapes=[
            pltpu.VMEM((RET_HEADS, RET_HEAD_DIM, RET_HEAD_DIM), F32),
            pltpu.VMEM((tm + V7X_SUBLANES, CONV_DIM), F32),
        ],
        compiler_params=pltpu.CompilerParams(
            dimension_semantics=("arbitrary", "arbitrary"),
            vmem_limit_bytes=56 * 1024 * 1024),
        name="even_mix",
    )(x, mod, norm_g, w_in, conv_w, ret_g, rot, dec, qdec, kdec, sdec, *[w for w, _ in cast])
    return outs[0], outs[1:]


def _qkv_kernel(x_ref, mod_ref, ng_ref, w_ref, qg_ref, kg_ref, q_ref, k_ref, v_ref):
    b = pl.program_id(0)
    x = x_ref[0]
    h = _norm_mod(x, ng_ref[0], _mod_row(mod_ref, 1, b), _mod_row(mod_ref, 0, b)).astype(BF16)
    qkv = jnp.dot(h, w_ref[0], preferred_element_type=F32)
    qg = qg_ref[0] * (SB_HEAD_DIM ** -0.5 * LOG2E)
    kg = kg_ref[0]
    for hh in range(SB_HEADS):
        c0 = hh * SB_HEAD_DIM
        q = qkv[:, c0:c0 + SB_HEAD_DIM]
        k = qkv[:, SB_DIM + c0:SB_DIM + c0 + SB_HEAD_DIM]
        v = qkv[:, 2 * SB_DIM + c0:2 * SB_DIM + c0 + SB_HEAD_DIM]
        qn = (q * lax.rsqrt(jnp.mean(q * q, axis=-1, keepdims=True) + EPS)) * qg
        kn = (k * lax.rsqrt(jnp.mean(k * k, axis=-1, keepdims=True) + EPS)) * kg
        q_ref[0, hh] = qn.astype(BF16)
        k_ref[0, hh] = kn.astype(BF16)
        v_ref[0, hh] = v.astype(BF16)


def _qkv(x, mod, l, j, norm_g, w_qkv, q_g, k_g):
    tm = QKV_TM
    layer = lambda b, s: (l, 0, 0)
    odd = lambda b, s: (j, 0, 0)
    head_spec = pl.BlockSpec((1, SB_HEADS, tm, SB_HEAD_DIM), lambda b, s: (b, 0, s, 0))
    head_shape = jax.ShapeDtypeStruct((BATCH, SB_HEADS, SEQ, SB_HEAD_DIM), BF16)
    return pl.pallas_call(
        _qkv_kernel,
        grid=(BATCH, SEQ // tm),
        in_specs=[
            pl.BlockSpec((1, tm, D_MODEL), lambda b, s: (b, s, 0)),
            pl.BlockSpec((1, 6, V7X_SUBLANES, D_MODEL), lambda b, s: (l, 0, 0, 0)),
            pl.BlockSpec((1, 1, D_MODEL), layer),
            pl.BlockSpec((1, D_MODEL, 3 * SB_DIM), lambda b, s: (0, 0, 0),
                         pipeline_mode=pl.Buffered(1)),
            pl.BlockSpec((1, 1, SB_HEAD_DIM), odd),
            pl.BlockSpec((1, 1, SB_HEAD_DIM), odd),
        ],
        out_specs=[head_spec, head_spec, head_spec],
        out_shape=[head_shape, head_shape, head_shape],
        compiler_params=pltpu.CompilerParams(
            dimension_semantics=("parallel", "parallel"),
            vmem_limit_bytes=56 * 1024 * 1024),
        name="qkv",
    )(x, mod, norm_g, w_qkv, q_g, k_g)


def _softplus2(z):
    return jnp.maximum(z, jnp.log(1.0 + jnp.exp2(jnp.minimum(z, SOFTPLUS_LINEAR))) * LOG2E)


def _sb_kernel(q_ref, k_ref, v_ref, o_ref, spb0, spb1, lb0, lb1, a0, a1, rc_ref, ac_ref,
               acc_all, r_all):
    row = lax.broadcasted_iota(jnp.int32, (SB_TK, SB_TK), 0)
    col = lax.broadcasted_iota(jnp.int32, (SB_TK, SB_TK), 1)
    later = (row > col).astype(BF16)
    causal = col < row
    sets = ((spb0, lb0), (spb1, lb1))
    abuf = (a0, a1)
    nblk = SEQ // SB_TK

    def blk(ref, j, rows=SB_TK):
        return ref[0, 0, pl.ds(pl.multiple_of(j * SB_TK, SB_TK), rows), :]

    def qk(q, j):
        return lax.dot_general(q, blk(k_ref, j), (((1,), (1,)), ((), ())),
                               preferred_element_type=F32)

    def stage_a(j, i, m):
        spb_ref, lb_ref = sets[i]
        z = qk(blk(q_ref, j, m), j)
        z_top = jnp.where(causal, z[:SB_TK], MASKED_LOGIT)
        sp_top = _softplus2(z_top)
        spb_ref[0:SB_TK] = sp_top.astype(BF16)
        lb_ref[0:SB_TK] = z_top - sp_top
        if m > SB_TK:
            sp = _softplus2(z[SB_TK:])
            spb_ref[SB_TK:m] = sp.astype(BF16)
            lb_ref[SB_TK:m] = z[SB_TK:] - sp

    def stage_b(j, i, m):
        spb_ref, lb_ref = sets[i]
        cs = jnp.dot(spb_ref[0:m], later, preferred_element_type=F32)
        rowsum = cs[:, 0:1] + spb_ref[0:m, 0:1].astype(F32)
        abuf[i][0:SB_TK] = lb_ref[0:SB_TK] - cs[:SB_TK]
        if m > SB_TK:
            r_old = rc_ref[...]
            abuf[i][SB_TK:m] = (lb_ref[SB_TK:m] - cs[SB_TK:]) - r_old
            r_all[pl.ds(pl.multiple_of((j + 1) * SB_TK, SB_TK), SB_TK), :] = r_old + rowsum[SB_TK:]
        rc_ref[...] = rowsum[:SB_TK]

    def stage_c(j, i, m):
        w = jnp.exp2(abuf[i][0:m]).astype(BF16)
        pv = jnp.dot(w, blk(v_ref, j), preferred_element_type=F32)
        if m > SB_TK:
            acc_all[pl.ds(pl.multiple_of((j + 1) * SB_TK, SB_TK), SB_TK), :] = ac_ref[...] + pv[SB_TK:]
        ac_ref[...] = pv[:SB_TK]

    def tick(tau, par, a=True, b=True, c=True):
        rows = lambda t: SB_TK if (isinstance(t, int) and t == 0) else 2 * SB_TK
        if a:
            stage_a(nblk - 1 - tau, par, rows(tau))
        if b:
            stage_b(nblk - tau, 1 - par, rows(tau - 1))
        if c:
            stage_c(nblk + 1 - tau, par, rows(tau - 2))

    for tau in range(nblk + 2):
        tick(tau, tau % 2, a=tau < nblk, b=1 <= tau <= nblk, c=tau >= 2)
    acc_all[0:SB_TK] = ac_ref[...]

    r_all[0:2 * SB_TK] = jnp.full((2 * SB_TK, 1), 2 * SB_EXIT_LOG2, F32)

    @pl.when(jnp.min(r_all[...]) <= SB_EXIT_LOG2)
    def _():
        def per_block(m, carry):
            rows = pl.ds(pl.multiple_of(m * SB_TK, SB_TK), SB_TK)

            def step(c2):
                j, _ = c2
                z = qk(blk(q_ref, m), j)
                sp = _softplus2(z)
                spb = sp.astype(BF16)
                cs = jnp.dot(spb, later, preferred_element_type=F32)
                r = r_all[rows, :]
                w = jnp.exp2(((z - sp) - cs) - r).astype(BF16)
                acc_all[rows, :] += jnp.dot(w, blk(v_ref, j), preferred_element_type=F32)
                r = r + (cs[:, 0:1] + spb[:, 0:1].astype(F32))
                r_all[rows, :] = r
                return j - 1, jnp.min(r)

            def more(c2):
                j, r_min = c2
                return jnp.logical_and(j >= 0, r_min <= SB_EXIT_LOG2)

            lax.while_loop(more, step, (m - 2, jnp.min(r_all[rows, :])))
            return carry

        lax.fori_loop(2, nblk, per_block, 0)

    o_ref[0, 0] = acc_all[...].astype(BF16)


def _sb_attention(q, k, v):
    head_spec = pl.BlockSpec((1, 1, SEQ, SB_HEAD_DIM), lambda b, h: (b, h, 0, 0))
    return pl.pallas_call(
        _sb_kernel,
        grid=(BATCH, SB_HEADS),
        in_specs=[head_spec, head_spec, head_spec],
        out_specs=head_spec,
        out_shape=jax.ShapeDtypeStruct((BATCH, SB_HEADS, SEQ, SB_HEAD_DIM), BF16),
        scratch_shapes=[
            pltpu.VMEM((2 * SB_TK, SB_TK), BF16), pltpu.VMEM((2 * SB_TK, SB_TK), BF16),
            pltpu.VMEM((2 * SB_TK, SB_TK), F32), pltpu.VMEM((2 * SB_TK, SB_TK), F32),
            pltpu.VMEM((2 * SB_TK, SB_TK), F32), pltpu.VMEM((2 * SB_TK, SB_TK), F32),
            pltpu.VMEM((SB_TK, 1), F32), pltpu.VMEM((SB_TK, SB_HEAD_DIM), F32),
            pltpu.VMEM((SEQ, SB_HEAD_DIM), F32), pltpu.VMEM((SEQ, 1), F32),
        ],
        compiler_params=pltpu.CompilerParams(dimension_semantics=("parallel", "parallel")),
        name="sb_attn",
    )(q, k, v)


def _post_kernel(x_ref, mix_ref, mod_ref, wo_ref, ng_ref, wg_ref, wu_ref, wd_ref, *rest):
    n_cast = (len(rest) - 1) // 2
    o_ref = rest[n_cast]
    b = pl.program_id(0)
    x = x_ref[0]
    if len(mix_ref.shape) == 4:
        mix = jnp.concatenate([mix_ref[0, hh] for hh in range(mix_ref.shape[1])], axis=1)
    else:
        mix = mix_ref[0]
    y = jnp.dot(mix, wo_ref[0], preferred_element_type=F32)
    x1 = x + _mod_row(mod_ref, 2, b) * y
    h = _norm_mod(x1, ng_ref[0], _mod_row(mod_ref, 4, b), _mod_row(mod_ref, 3, b)).astype(BF16)
    gate = jnp.dot(h, wg_ref[0], preferred_element_type=F32)
    up = jnp.dot(h, wu_ref[0], preferred_element_type=F32)
    act = ((gate * _sigmoid(gate)) * up).astype(BF16)
    ffn = jnp.dot(act, wd_ref[0], preferred_element_type=F32)
    o_ref[0] = x1 + _mod_row(mod_ref, 5, b) * ffn
    _cast_blocks(rest[:n_cast], rest[n_cast + 1:])


def _post(x, mix, mod, l, w_out, norm_g, w_gate, w_up, w_down, cast_next=()):
    tm = POST_TM
    n_s = SEQ // tm
    layer = lambda b, s: (l, 0, 0)
    first = lambda b, s: (0, 0, 0)
    single = pl.Buffered(1)
    cast_in, cast_out, cast_shapes = _cast_specs(cast_next, lambda b, s: b * n_s + s, BATCH * n_s)
    if mix.ndim == 4:
        mix_spec = pl.BlockSpec((1, mix.shape[1], tm, mix.shape[3]), lambda b, s: (b, 0, s, 0))
    else:
        mix_spec = pl.BlockSpec((1, tm, D_MODEL), lambda b, s: (b, s, 0))
    outs = pl.pallas_call(
        _post_kernel,
        grid=(BATCH, n_s),
        in_specs=[
            pl.BlockSpec((1, tm, D_MODEL), lambda b, s: (b, s, 0)),
            mix_spec,
            pl.BlockSpec((1, 6, V7X_SUBLANES, D_MODEL), lambda b, s: (l, 0, 0, 0)),
            pl.BlockSpec((1, D_MODEL, D_MODEL), first, pipeline_mode=single),
            pl.BlockSpec((1, 1, D_MODEL), layer),
            pl.BlockSpec((1, D_MODEL, D_FF), first, pipeline_mode=single),
            pl.BlockSpec((1, D_MODEL, D_FF), first, pipeline_mode=single),
            pl.BlockSpec((1, D_FF, D_MODEL), first, pipeline_mode=single),
        ] + cast_in,
        out_specs=[pl.BlockSpec((1, tm, D_MODEL), lambda b, s: (b, s, 0))] + cast_out,
        out_shape=[jax.ShapeDtypeStruct((BATCH, SEQ, D_MODEL), F32)] + cast_shapes,
        compiler_params=pltpu.CompilerParams(
            dimension_semantics=("arbitrary", "arbitrary"),
            vmem_limit_bytes=58 * 1024 * 1024),
        name="post",
    )(x, mix, mod, w_out, norm_g, w_gate, w_up, w_down, *[w for w, _ in cast_next])
    return outs[0], outs[1:]


def kernel(x, c, ada_w, ada_b, norm_mix_g, norm_ffn_g, ev_w_in, ev_conv_w, ev_ret_norm_g, ev_w_out,
           od_w_qkv, od_q_norm_g, od_k_norm_g, od_w_out, ffn_w_gate, ffn_w_up, ffn_w_down):
    row3 = lambda g: g.reshape(g.shape[0], 1, g.shape[1])
    norm_mix_g, norm_ffn_g = row3(norm_mix_g), row3(norm_ffn_g)
    ev_ret_norm_g, od_q_norm_g, od_k_norm_g = row3(ev_ret_norm_g), row3(od_q_norm_g), row3(od_k_norm_g)
    ev_conv_w = jnp.pad(ev_conv_w, ((0, 0), (0, V7X_SUBLANES - CONV_WIDTH), (0, 0)))
    tables = _retention_tables()

    def f32_weights(l):
        j = l // 2
        mixer = (ev_w_in, ev_w_out) if l % 2 == 0 else (od_w_qkv, od_w_out)
        return [(mixer[0], j), (mixer[1], j), (ffn_w_gate, l), (ffn_w_up, l), (ffn_w_down, l)]

    mod, weights = _adaln(c, ada_w, ada_b, f32_weights(0)[:2])
    for l in range(DEPTH):
        j = l // 2
        w_in, w_out, *ffn = weights
        if l % 2 == 0:
            mix, ffn0 = _even_mix(x, mod, l, j, norm_mix_g, w_in, ev_conv_w, ev_ret_norm_g, tables,
                                  cast=f32_weights(0)[2:] if l == 0 else ())
            ffn = ffn or ffn0
        else:
            q, k, v = _qkv(x, mod, l, j, norm_mix_g, w_in, od_q_norm_g, od_k_norm_g)
            mix = _sb_attention(q, k, v)
        w_gate, w_up, w_down = ffn
        cast_next = f32_weights(l + 1) if l + 1 < DEPTH else ()
        x, weights = _post(x, mix, mod, l, w_out, norm_ffn_g, w_gate, w_up, w_down, cast_next)
    return x
```

```python
import functools
import math

import jax
import jax.numpy as jnp
import numpy as np
from jax import lax
from jax.experimental import pallas as pl
from jax.experimental.pallas import tpu as pltpu

D_MODEL = 1024
BATCH = 4
SEQ = 4096
DEPTH = 4
CHUNK = 64
EPS = 1e-6
CONV_WIDTH = 3
CONV_DIM = 512
RET_HEADS = 4
RET_HEAD_DIM = 128
RET_DIM = 512
ROPE_THETA = 10000.0
SB_HEADS = 8
SB_HEAD_DIM = 128
SB_DIM = 1024
D_FF = 2816
EVEN_IN = 3 * CONV_DIM + 4 * RET_DIM

V7X_SUBLANES = 8
V7X_BF16_SUBLANES = 16
V7X_LANES = 128
V7X_MXU_DIM = 256
V7X_VMEM_BYTES = 64 * 1024 * 1024

LOG2E = 1.0 / math.log(2.0)

EVEN_TM = 1024
RET_TILE = V7X_MXU_DIM
QKV_TM = 1024
POST_TM = 512
SB_TK = V7X_MXU_DIM
MASKED_LOGIT = -1e30
SOFTPLUS_LINEAR = 64.0
F32_MIN_SUBNORMAL_LOG2 = -149.0
SB_EXIT_LOG2 = 160.0
assert SB_EXIT_LOG2 > -F32_MIN_SUBNORMAL_LOG2 + 1

F32 = jnp.float32
BF16 = jnp.bfloat16


def _sigmoid(x):
    return 1.0 / (1.0 + jnp.exp(-x))


def _norm_mod(x, gain, scale, shift):
    ms = jnp.mean(x * x, axis=-1, keepdims=True)
    return (x * lax.rsqrt(ms + EPS)) * (gain * (1.0 + scale)) + shift


def _mod_row(mod_ref, k, b):
    return mod_ref[0, k, pl.ds(b, 1), :]


def _cast_specs(cast, step_of, n_steps):
    in_specs, out_specs, shapes = [], [], []
    for w, li in cast:
        _, rows, cols = w.shape
        n = 1
        while 2 * n <= n_steps and rows % (2 * n * V7X_BF16_SUBLANES) == 0:
            n *= 2
        chunk = lambda *ids, n=n: jnp.minimum(step_of(*ids), n - 1)
        in_specs.append(pl.BlockSpec((1, rows // n, cols),
                                     lambda *ids, li=li, chunk=chunk: (li, chunk(*ids), 0)))
        out_specs.append(pl.BlockSpec((1, rows // n, cols),
                                      lambda *ids, chunk=chunk: (0, chunk(*ids), 0)))
        shapes.append(jax.ShapeDtypeStruct((1, rows, cols), BF16))
    return in_specs, out_specs, shapes


def _cast_blocks(src_refs, dst_refs):
    for src_ref, dst_ref in zip(src_refs, dst_refs):
        dst_ref[...] = src_ref[...].astype(BF16)


def _adaln_kernel(ct_ref, w_ref, b_ref, *rest):
    n_cast = (len(rest) - 1) // 2
    o_ref = rest[n_cast]
    ct = ct_ref[...]
    cat = ct * _sigmoid(ct)
    w = w_ref[0]
    rows = [jnp.sum(w * cat[:, b:b + 1], axis=0, keepdims=True) for b in range(BATCH)]
    rows += [jnp.zeros((V7X_SUBLANES - BATCH, D_MODEL), F32)]
    o_ref[0, 0] = jnp.concatenate(rows, axis=0) + b_ref[0, 0]
    _cast_blocks(rest[:n_cast], rest[n_cast + 1:])


def _adaln(c, ada_w, ada_b, cast_first=()):
    c_pad = jnp.pad(c, ((0, V7X_SUBLANES - BATCH), (0, 0))).T
    b4 = ada_b.reshape(DEPTH, 6, 1, D_MODEL)
    cast_in, cast_out, cast_shapes = _cast_specs(cast_first, lambda l, k: l * 6 + k, DEPTH * 6)
    outs = pl.pallas_call(
        _adaln_kernel,
        grid=(DEPTH, 6),
        in_specs=[
            pl.BlockSpec((D_MODEL, V7X_SUBLANES), lambda l, k: (0, 0)),
            pl.BlockSpec((1, D_MODEL, D_MODEL), lambda l, k: (l, 0, k)),
            pl.BlockSpec((1, 1, 1, D_MODEL), lambda l, k: (l, k, 0, 0)),
        ] + cast_in,
        out_specs=[pl.BlockSpec((1, 1, V7X_SUBLANES, D_MODEL), lambda l, k: (l, k, 0, 0))] + cast_out,
        out_shape=[jax.ShapeDtypeStruct((DEPTH, 6, V7X_SUBLANES, D_MODEL), F32)] + cast_shapes,
        compiler_params=pltpu.CompilerParams(dimension_semantics=("arbitrary", "arbitrary")),
        name="adaln",
    )(c_pad, ada_w, b4, *[w for w, _ in cast_first])
    return outs[0], outs[1:]


def _retention_tables():
    h = np.arange(RET_HEADS, dtype=np.float64)
    log_g = np.log1p(-np.exp2(-5.0 - h))
    idx = np.arange(RET_TILE, dtype=np.float64)
    diff = idx[:, None] - idx[None, :]
    same_or_earlier = (idx[None, :] // CHUNK) <= (idx[:, None] // CHUNK)
    k_scale = RET_HEAD_DIM ** -0.5
    dec = np.exp(np.abs(diff)[None] * log_g[:, None, None]) * same_or_earlier[None] * k_scale
    qdec = np.exp((idx + 1.0)[None, :] * log_g[:, None])
    kdec = np.exp((RET_TILE - 1.0 - idx)[None, :] * log_g[:, None]) * k_scale
    sdec = np.exp(RET_TILE * log_g)
    qdec = np.broadcast_to(qdec[:, :, None], (RET_HEADS, RET_TILE, V7X_LANES))
    kdec = np.broadcast_to(kdec[:, :, None], (RET_HEADS, RET_TILE, V7X_LANES))
    sdec = np.broadcast_to(sdec[:, None, None], (RET_HEADS, V7X_SUBLANES, V7X_LANES))
    inv_freq = 1.0 / (ROPE_THETA ** (np.arange(0, RET_HEAD_DIM, 2, dtype=np.float64) / RET_HEAD_DIM))
    ang = np.arange(SEQ, dtype=np.float64)[:, None] * inv_freq[None, :]
    cos2 = np.concatenate([np.cos(ang), np.cos(ang)], axis=-1)
    sin2 = np.concatenate([-np.sin(ang), np.sin(ang)], axis=-1)
    rot = np.stack([cos2, sin2])
    f = lambda a: jnp.asarray(np.ascontiguousarray(a), dtype=F32)
    return f(dec), f(qdec), f(kdec), f(sdec), f(rot)


def _even_kernel(x_ref, mod_ref, ng_ref, win_ref, cw_ref, rg_ref, rot_ref, dec_ref, qdec_ref,
                 kdec_ref, sdec_ref, *rest):
    n_cast = (len(rest) - 3) // 2
    mix_ref = rest[n_cast]
    state_ref, zbuf_ref = rest[-2:]
    _cast_blocks(rest[:n_cast], rest[n_cast + 1:2 * n_cast + 1])
    b = pl.program_id(0)
    si = pl.program_id(1)
    tm = EVEN_TM

    @pl.when(si == 0)
    def _():
        state_ref[...] = jnp.zeros_like(state_ref)
        zbuf_ref[0:V7X_SUBLANES, :] = jnp.zeros((V7X_SUBLANES, CONV_DIM), F32)

    q_off = 3 * CONV_DIM
    k_off = q_off + RET_DIM
    v_off = k_off + RET_DIM
    g_off = v_off + RET_DIM
    cw = cw_ref[0]
    sub_tiles = [slice(r0, r0 + RET_TILE) for r0 in range(0, tm, RET_TILE)]

    projs = []
    for rows in sub_tiles:
        h = _norm_mod(x_ref[0, rows, :], ng_ref[0], _mod_row(mod_ref, 1, b),
                      _mod_row(mod_ref, 0, b)).astype(BF16)
        projs.append(jnp.dot(h, win_ref[0], preferred_element_type=F32))

    for rows, proj in zip(sub_tiles, projs):
        r0 = rows.start
        zrows = lambda back: slice(V7X_SUBLANES + r0 - back, V7X_SUBLANES + r0 - back + RET_TILE)
        z = proj[:, CONV_DIM:2 * CONV_DIM] * proj[:, 2 * CONV_DIM:3 * CONV_DIM]
        zbuf_ref[zrows(0), :] = z
        y = cw[0:1, :] * zbuf_ref[zrows(2), :] + cw[1:2, :] * zbuf_ref[zrows(1), :] + cw[2:3, :] * z
        mix_ref[0, rows, 0:CONV_DIM] = (proj[:, 0:CONV_DIM] * y).astype(BF16)

        cos = rot_ref[0, rows, :]
        sin = rot_ref[1, rows, :]
        for hh in range(RET_HEADS):
            c0 = hh * RET_HEAD_DIM
            q = proj[:, q_off + c0:q_off + c0 + RET_HEAD_DIM]
            k = proj[:, k_off + c0:k_off + c0 + RET_HEAD_DIM]
            v = proj[:, v_off + c0:v_off + c0 + RET_HEAD_DIM].astype(BF16)
            g = proj[:, g_off + c0:g_off + c0 + RET_HEAD_DIM]
            qr = q * cos + pltpu.roll(q, RET_HEAD_DIM // 2, axis=1) * sin
            kr = k * cos + pltpu.roll(k, RET_HEAD_DIM // 2, axis=1) * sin
            s = lax.dot_general(qr.astype(BF16), kr.astype(BF16), (((1,), (1,)), ((), ())),
                                preferred_element_type=F32)
            s = s * dec_ref[hh]
            o = jnp.dot(s.astype(BF16), v, preferred_element_type=F32)
            st = state_ref[hh]
            o = o + jnp.dot((qr * qdec_ref[hh]).astype(BF16), st.astype(BF16),
                            preferred_element_type=F32)
            kd = (kr * kdec_ref[hh]).astype(BF16)
            kv = lax.dot_general(kd, v, (((0,), (0,)), ((), ())), preferred_element_type=F32)
            state_ref[hh] = st * sdec_ref[hh, 0:1, :] + kv
            ms = jnp.mean(o * o, axis=-1, keepdims=True)
            rn = (o * lax.rsqrt(ms + EPS)) * rg_ref[0, :, c0:c0 + RET_HEAD_DIM]
            out = (g * _sigmoid(g)) * rn
            mix_ref[0, rows, CONV_DIM + c0:CONV_DIM + c0 + RET_HEAD_DIM] = out.astype(BF16)
    zbuf_ref[0:V7X_SUBLANES, :] = zbuf_ref[tm:tm + V7X_SUBLANES, :]


def _even_mix(x, mod, l, j, norm_g, w_in, conv_w, ret_g, tables, cast=()):
    dec, qdec, kdec, sdec, rot = tables
    tm = EVEN_TM
    n_s = SEQ // tm
    const3 = lambda b, s: (0, 0, 0)
    layer = lambda b, s: (l, 0, 0)
    even = lambda b, s: (j, 0, 0)
    cast_in, cast_out, cast_shapes = _cast_specs(cast, lambda b, s: b * n_s + s, BATCH * n_s)
    outs = pl.pallas_call(
        _even_kernel,
        grid=(BATCH, SEQ // tm),
        in_specs=[
            pl.BlockSpec((1, tm, D_MODEL), lambda b, s: (b, s, 0)),
            pl.BlockSpec((1, 6, V7X_SUBLANES, D_MODEL), lambda b, s: (l, 0, 0, 0)),
            pl.BlockSpec((1, 1, D_MODEL), layer),
            pl.BlockSpec((1, D_MODEL, EVEN_IN), const3, pipeline_mode=pl.Buffered(1)),
            pl.BlockSpec((1, V7X_SUBLANES, CONV_DIM), even),
            pl.BlockSpec((1, 1, RET_DIM), even),
            pl.BlockSpec((2, tm, RET_HEAD_DIM), lambda b, s: (0, s, 0)),
            pl.BlockSpec((RET_HEADS, RET_TILE, RET_TILE), const3),
            pl.BlockSpec((RET_HEADS, RET_TILE, V7X_LANES), const3),
            pl.BlockSpec((RET_HEADS, RET_TILE, V7X_LANES), const3),
            pl.BlockSpec((RET_HEADS, V7X_SUBLANES, V7X_LANES), const3),
        ] + cast_in,
        out_specs=[pl.BlockSpec((1, tm, D_MODEL), lambda b, s: (b, s, 0))] + cast_out,
        out_shape=[jax.ShapeDtypeStruct((BATCH, SEQ, D_MODEL), BF16)] + cast_shapes,
        scratch_shapes=[
            pltpu.VMEM((RET_HEADS, RET_HEAD_DIM, RET_HEAD_DIM), F32),
            pltpu.VMEM((tm + V7X_SUBLANES, CONV_DIM), F32),
        ],
        compiler_params=pltpu.CompilerParams(
            dimension_semantics=("arbitrary", "arbitrary"),
            vmem_limit_bytes=56 * 1024 * 1024),
        name="even_mix",
    )(x, mod, norm_g, w_in, conv_w, ret_g, rot, dec, qdec, kdec, sdec, *[w for w, _ in cast])
    return outs[0], outs[1:]


def _qkv_kernel(x_ref, mod_ref, ng_ref, w_ref, qg_ref, kg_ref, q_ref, k_ref, v_ref):
    b = pl.program_id(0)
    x = x_ref[0]
    h = _norm_mod(x, ng_ref[0], _mod_row(mod_ref, 1, b), _mod_row(mod_ref, 0, b)).astype(BF16)
    qkv = jnp.dot(h, w_ref[0], preferred_element_type=F32)
    qg = qg_ref[0] * (SB_HEAD_DIM ** -0.5 * LOG2E)
    kg = kg_ref[0]
    for hh in range(SB_HEADS):
        c0 = hh * SB_HEAD_DIM
        q = qkv[:, c0:c0 + SB_HEAD_DIM]
        k = qkv[:, SB_DIM + c0:SB_DIM + c0 + SB_HEAD_DIM]
        v = qkv[:, 2 * SB_DIM + c0:2 * SB_DIM + c0 + SB_HEAD_DIM]
        qn = (q * lax.rsqrt(jnp.mean(q * q, axis=-1, keepdims=True) + EPS)) * qg
        kn = (k * lax.rsqrt(jnp.mean(k * k, axis=-1, keepdims=True) + EPS)) * kg
        q_ref[0, hh] = qn.astype(BF16)
        k_ref[0, hh] = kn.astype(BF16)
        v_ref[0, hh] = v.astype(BF16)


def _qkv(x, mod, l, j, norm_g, w_qkv, q_g, k_g):
    tm = QKV_TM
    layer = lambda b, s: (l, 0, 0)
    odd = lambda b, s: (j, 0, 0)
    head_spec = pl.BlockSpec((1, SB_HEADS, tm, SB_HEAD_DIM), lambda b, s: (b, 0, s, 0))
    head_shape = jax.ShapeDtypeStruct((BATCH, SB_HEADS, SEQ, SB_HEAD_DIM), BF16)
    return pl.pallas_call(
        _qkv_kernel,
        grid=(BATCH, SEQ // tm),
        in_specs=[
            pl.BlockSpec((1, tm, D_MODEL), lambda b, s: (b, s, 0)),
            pl.BlockSpec((1, 6, V7X_SUBLANES, D_MODEL), lambda b, s: (l, 0, 0, 0)),
            pl.BlockSpec((1, 1, D_MODEL), layer),
            pl.BlockSpec((1, D_MODEL, 3 * SB_DIM), lambda b, s: (0, 0, 0),
                         pipeline_mode=pl.Buffered(1)),
            pl.BlockSpec((1, 1, SB_HEAD_DIM), odd),
            pl.BlockSpec((1, 1, SB_HEAD_DIM), odd),
        ],
        out_specs=[head_spec, head_spec, head_spec],
        out_shape=[head_shape, head_shape, head_shape],
        compiler_params=pltpu.CompilerParams(
            dimension_semantics=("parallel", "parallel"),
            vmem_limit_bytes=56 * 1024 * 1024),
        name="qkv",
    )(x, mod, norm_g, w_qkv, q_g, k_g)


def _softplus2(z):
    return jnp.maximum(z, jnp.log(1.0 + jnp.exp2(jnp.minimum(z, SOFTPLUS_LINEAR))) * LOG2E)


def _sb_kernel(q_ref, k_ref, v_ref, o_ref, spb0, spb1, lb0, lb1, a0, a1, rc_ref, ac_ref,
               acc_all, r_all):
    row = lax.broadcasted_iota(jnp.int32, (SB_TK, SB_TK), 0)
    col = lax.broadcasted_iota(jnp.int32, (SB_TK, SB_TK), 1)
    later = (row > col).astype(BF16)
    causal = col < row
    sets = ((spb0, lb0), (spb1, lb1))
    abuf = (a0, a1)
    nblk = SEQ // SB_TK

    def blk(ref, j, rows=SB_TK):
        return ref[0, 0, pl.ds(pl.multiple_of(j * SB_TK, SB_TK), rows), :]

    def qk(q, j):
        return lax.dot_general(q, blk(k_ref, j), (((1,), (1,)), ((), ())),
                               preferred_element_type=F32)

    def stage_a(j, i, m):
        spb_ref, lb_ref = sets[i]
        z = qk(blk(q_ref, j, m), j)
        z_top = jnp.where(causal, z[:SB_TK], MASKED_LOGIT)
        sp_top = _softplus2(z_top)
        spb_ref[0:SB_TK] = sp_top.astype(BF16)
        lb_ref[0:SB_TK] = z_top - sp_top
        if m > SB_TK:
            sp = _softplus2(z[SB_TK:])
            spb_ref[SB_TK:m] = sp.astype(BF16)
            lb_ref[SB_TK:m] = z[SB_TK:] - sp

    def stage_b(j, i, m):
        spb_ref, lb_ref = sets[i]
        cs = jnp.dot(spb_ref[0:m], later, preferred_element_type=F32)
        rowsum = cs[:, 0:1] + spb_ref[0:m, 0:1].astype(F32)
        abuf[i][0:SB_TK] = lb_ref[0:SB_TK] - cs[:SB_TK]
        if m > SB_TK:
            r_old = rc_ref[...]
            abuf[i][SB_TK:m] = (lb_ref[SB_TK:m] - cs[SB_TK:]) - r_old
            r_all[pl.ds(pl.multiple_of((j + 1) * SB_TK, SB_TK), SB_TK), :] = r_old + rowsum[SB_TK:]
        rc_ref[...] = rowsum[:SB_TK]

    def stage_c(j, i, m):
        w = jnp.exp2(abuf[i][0:m]).astype(BF16)
        pv = jnp.dot(w, blk(v_ref, j), preferred_element_type=F32)
        if m > SB_TK:
            acc_all[pl.ds(pl.multiple_of((j + 1) * SB_TK, SB_TK), SB_TK), :] = ac_ref[...] + pv[SB_TK:]
        ac_ref[...] = pv[:SB_TK]

    def tick(tau, par, a=True, bc=True):
        rows = lambda t: SB_TK if (isinstance(t, int) and t == 0) else 2 * SB_TK
        if a:
            stage_a(nblk - 1 - tau, par, rows(tau))
        if bc:
            stage_b(nblk - tau, 1 - par, rows(tau - 1))
            stage_c(nblk - tau, 1 - par, rows(tau - 1))

    for tau in range(nblk + 1):
        tick(tau, tau % 2, a=tau < nblk, bc=tau >= 1)
    acc_all[0:SB_TK] = ac_ref[...]

    r_all[0:2 * SB_TK] = jnp.full((2 * SB_TK, 1), 2 * SB_EXIT_LOG2, F32)

    @pl.when(jnp.min(r_all[...]) <= SB_EXIT_LOG2)
    def _():
        def per_block(m, carry):
            rows = pl.ds(pl.multiple_of(m * SB_TK, SB_TK), SB_TK)

            def step(c2):
                j, _ = c2
                z = qk(blk(q_ref, m), j)
                sp = _softplus2(z)
                spb = sp.astype(BF16)
                cs = jnp.dot(spb, later, preferred_element_type=F32)
                r = r_all[rows, :]
                w = jnp.exp2(((z - sp) - cs) - r).astype(BF16)
                acc_all[rows, :] += jnp.dot(w, blk(v_ref, j), preferred_element_type=F32)
                r = r + (cs[:, 0:1] + spb[:, 0:1].astype(F32))
                r_all[rows, :] = r
                return j - 1, jnp.min(r)

            def more(c2):
                j, r_min = c2
                return jnp.logical_and(j >= 0, r_min <= SB_EXIT_LOG2)

            lax.while_loop(more, step, (m - 2, jnp.min(r_all[rows, :])))
            return carry

        lax.fori_loop(2, nblk, per_block, 0)

    o_ref[0] = acc_all[...].astype(BF16)


def _sb_attention(q, k, v):
    head_spec = pl.BlockSpec((1, 1, SEQ, SB_HEAD_DIM), lambda b, h: (b, h, 0, 0))
    return pl.pallas_call(
        _sb_kernel,
        grid=(BATCH, SB_HEADS),
        in_specs=[head_spec, head_spec, head_spec],
        out_specs=pl.BlockSpec((1, SEQ, SB_HEAD_DIM), lambda b, h: (b, 0, h)),
        out_shape=jax.ShapeDtypeStruct((BATCH, SEQ, SB_DIM), BF16),
        scratch_shapes=[
            pltpu.VMEM((2 * SB_TK, SB_TK), BF16), pltpu.VMEM((2 * SB_TK, SB_TK), BF16),
            pltpu.VMEM((2 * SB_TK, SB_TK), F32), pltpu.VMEM((2 * SB_TK, SB_TK), F32),
            pltpu.VMEM((2 * SB_TK, SB_TK), F32), pltpu.VMEM((2 * SB_TK, SB_TK), F32),
            pltpu.VMEM((SB_TK, 1), F32), pltpu.VMEM((SB_TK, SB_HEAD_DIM), F32),
            pltpu.VMEM((SEQ, SB_HEAD_DIM), F32), pltpu.VMEM((SEQ, 1), F32),
        ],
        compiler_params=pltpu.CompilerParams(dimension_semantics=("parallel", "parallel")),
        name="sb_attn",
    )(q, k, v)


def _post_kernel(x_ref, mix_ref, mod_ref, wo_ref, ng_ref, wg_ref, wu_ref, wd_ref, *rest):
    n_cast = (len(rest) - 1) // 2
    o_ref = rest[n_cast]
    b = pl.program_id(0)
    x = x_ref[0]
    y = jnp.dot(mix_ref[0], wo_ref[0], preferred_element_type=F32)
    x1 = x + _mod_row(mod_ref, 2, b) * y
    h = _norm_mod(x1, ng_ref[0], _mod_row(mod_ref, 4, b), _mod_row(mod_ref, 3, b)).astype(BF16)
    gate = jnp.dot(h, wg_ref[0], preferred_element_type=F32)
    up = jnp.dot(h, wu_ref[0], preferred_element_type=F32)
    act = ((gate * _sigmoid(gate)) * up).astype(BF16)
    ffn = jnp.dot(act, wd_ref[0], preferred_element_type=F32)
    o_ref[0] = x1 + _mod_row(mod_ref, 5, b) * ffn
    _cast_blocks(rest[:n_cast], rest[n_cast + 1:])


def _post(x, mix, mod, l, w_out, norm_g, w_gate, w_up, w_down, cast_next=()):
    tm = POST_TM
    n_s = SEQ // tm
    layer = lambda b, s: (l, 0, 0)
    first = lambda b, s: (0, 0, 0)
    single = pl.Buffered(1)
    cast_in, cast_out, cast_shapes = _cast_specs(cast_next, lambda b, s: b * n_s + s, BATCH * n_s)
    outs = pl.pallas_call(
        _post_kernel,
        grid=(BATCH, n_s),
        in_specs=[
            pl.BlockSpec((1, tm, D_MODEL), lambda b, s: (b, s, 0)),
            pl.BlockSpec((1, tm, D_MODEL), lambda b, s: (b, s, 0)),
            pl.BlockSpec((1, 6, V7X_SUBLANES, D_MODEL), lambda b, s: (l, 0, 0, 0)),
            pl.BlockSpec((1, D_MODEL, D_MODEL), first, pipeline_mode=single),
            pl.BlockSpec((1, 1, D_MODEL), layer),
            pl.BlockSpec((1, D_MODEL, D_FF), first, pipeline_mode=single),
            pl.BlockSpec((1, D_MODEL, D_FF), first, pipeline_mode=single),
            pl.BlockSpec((1, D_FF, D_MODEL), first, pipeline_mode=single),
        ] + cast_in,
        out_specs=[pl.BlockSpec((1, tm, D_MODEL), lambda b, s: (b, s, 0))] + cast_out,
        out_shape=[jax.ShapeDtypeStruct((BATCH, SEQ, D_MODEL), F32)] + cast_shapes,
        compiler_params=pltpu.CompilerParams(
            dimension_semantics=("arbitrary", "arbitrary"),
            vmem_limit_bytes=58 * 1024 * 1024),
        name="post",
    )(x, mix, mod, w_out, norm_g, w_gate, w_up, w_down, *[w for w, _ in cast_next])
    return outs[0], outs[1:]


def kernel(x, c, ada_w, ada_b, norm_mix_g, norm_ffn_g, ev_w_in, ev_conv_w, ev_ret_norm_g, ev_w_out,
           od_w_qkv, od_q_norm_g, od_k_norm_g, od_w_out, ffn_w_gate, ffn_w_up, ffn_w_down):
    row3 = lambda g: g.reshape(g.shape[0], 1, g.shape[1])
    norm_mix_g, norm_ffn_g = row3(norm_mix_g), row3(norm_ffn_g)
    ev_ret_norm_g, od_q_norm_g, od_k_norm_g = row3(ev_ret_norm_g), row3(od_q_norm_g), row3(od_k_norm_g)
    ev_conv_w = jnp.pad(ev_conv_w, ((0, 0), (0, V7X_SUBLANES - CONV_WIDTH), (0, 0)))
    tables = _retention_tables()

    def f32_weights(l):
        j = l // 2
        mixer = (ev_w_in, ev_w_out) if l % 2 == 0 else (od_w_qkv, od_w_out)
        return [(mixer[0], j), (mixer[1], j), (ffn_w_gate, l), (ffn_w_up, l), (ffn_w_down, l)]

    mod, weights = _adaln(c, ada_w, ada_b, f32_weights(0)[:2])
    for l in range(DEPTH):
        j = l // 2
        w_in, w_out, *ffn = weights
        if l % 2 == 0:
            mix, ffn0 = _even_mix(x, mod, l, j, norm_mix_g, w_in, ev_conv_w, ev_ret_norm_g, tables,
                                  cast=f32_weights(0)[2:] if l == 0 else ())
            ffn = ffn or ffn0
        else:
            q, k, v = _qkv(x, mod, l, j, norm_mix_g, w_in, od_q_norm_g, od_k_norm_g)
            mix = _sb_attention(q, k, v)
        w_gate, w_up, w_down = ffn
        cast_next = f32_weights(l + 1) if l + 1 < DEPTH else ()
        x, weights = _post(x, mix, mod, l, w_out, norm_ffn_g, w_gate, w_up, w_down, cast_next)
    return x
```
